```python
import jax, jax.numpy as jnp
from jax import lax
import numpy as np

D_MODEL = 2048
BATCH = 4
SEQ = 4096
DEPTH = 1

GRID_W = 64
CTX_LEN = 256
MLA_HEADS = 16
QK_NOPE_DIM = 128
QK_ROPE_DIM = 64
V_HEAD_DIM = 128
Q_LORA_RANK = 768
KV_LORA_RANK = 512
ROPE_THETA = 10000.0
ROPE_FREQS_PER_AXIS = QK_ROPE_DIM // 4
SOFTMAX_SCALE = (QK_NOPE_DIM + QK_ROPE_DIM) ** -0.5
Q_BLOCK = 128
CONV_CHANNELS = D_MODEL // 2
CONV_WIDTH = 31
CONV_PAD = CONV_WIDTH // 2
D_FF = 4 * D_MODEL
N_BRANCHES = 2
PROJ_SPLITS = (KV_LORA_RANK,
               KV_LORA_RANK + QK_ROPE_DIM,
               KV_LORA_RANK + QK_ROPE_DIM + Q_LORA_RANK,
               KV_LORA_RANK + QK_ROPE_DIM + Q_LORA_RANK + 2 * CONV_CHANNELS)
PROJ_WIDTH = PROJ_SPLITS[-1] + N_BRANCHES * D_MODEL
N_MOD = 6
LN_EPS = 1e-6
DEEPNORM_ALPHA = (2 * DEPTH) ** 0.25
DEEPNORM_BETA = (8 * DEPTH) ** -0.25

kernel_name = "hybrid_conv_mla_dit_block"


def _layer_norm(x, gain=None, bias=None):
    xf = x.astype(jnp.float32)
    xc = xf - jnp.mean(xf, axis=-1, keepdims=True)
    y = xc * lax.rsqrt(jnp.mean(xc * xc, axis=-1, keepdims=True) + LN_EPS)
    if gain is not None:
        y = y * gain + bias
    return y.astype(x.dtype)


def _rms_norm(x, gain):
    xf = x.astype(jnp.float32)
    y = xf * lax.rsqrt(jnp.mean(xf * xf, axis=-1, keepdims=True) + LN_EPS) * gain
    return y.astype(x.dtype)


def _modulate(x, shift, scale):
    return _layer_norm(x) * (1 + scale) + shift


def _grid_rope_tables(n_tok):
    rows = n_tok // GRID_W
    row = jnp.repeat(jnp.arange(rows, dtype=jnp.float32), GRID_W)
    col = jnp.tile(jnp.arange(GRID_W, dtype=jnp.float32), rows)
    inv_freq = ROPE_THETA ** (-jnp.arange(ROPE_FREQS_PER_AXIS, dtype=jnp.float32) / ROPE_FREQS_PER_AXIS)
    ang = jnp.stack([row[:, None] * inv_freq, col[:, None] * inv_freq], axis=1)
    return jnp.cos(ang), jnp.sin(ang)


def _rope_2d(x, cos, sin):
    xr = x.reshape(x.shape[:-1] + (2, 2, ROPE_FREQS_PER_AXIS)).astype(jnp.float32)
    x1, x2 = xr[..., 0, :], xr[..., 1, :]
    c, s = cos[:, None], sin[:, None]
    out = jnp.stack([x1 * c - x2 * s, x1 * s + x2 * c], axis=-2)
    return out.reshape(x.shape).astype(x.dtype)


def _attention(q, k, v):
    b, lq, h, dqk = q.shape
    nb = lq // Q_BLOCK
    qb = q.reshape(b, nb, Q_BLOCK, h, dqk).swapaxes(0, 1)

    def block(qi):
        s = jnp.einsum('bqhd,bkhd->bhqk', qi, k, preferred_element_type=jnp.float32)
        p = jax.nn.softmax(s, axis=-1).astype(v.dtype)
        return jnp.einsum('bhqk,bkhd->bqhd', p, v)

    o = lax.map(block, qb)
    return o.swapaxes(0, 1).reshape(b, lq, h, v.shape[-1])


def _mla_keys_values(kv_lat, k_r, p, cos, sin):
    b, l, _ = kv_lat.shape
    kv = (_rms_norm(kv_lat, p['kv_norm_g']) @ p['w_ukv']).reshape(b, l, MLA_HEADS, QK_NOPE_DIM + V_HEAD_DIM)
    k_nope, v = jnp.split(kv, [QK_NOPE_DIM], axis=-1)
    k_pe = k_r.reshape(b, l, 1, QK_ROPE_DIM)
    if cos is not None:
        k_pe = _rope_2d(k_pe, cos, sin)
    k = jnp.concatenate([k_nope, jnp.broadcast_to(k_pe, (b, l, MLA_HEADS, QK_ROPE_DIM))], axis=-1)
    return k, v


def _mla_queries(q_lat, p, cos, sin):
    b, l, _ = q_lat.shape
    q = (_rms_norm(q_lat, p['q_norm_g']) @ p['w_uq']).reshape(b, l, MLA_HEADS, QK_NOPE_DIM + QK_ROPE_DIM)
    if cos is not None:
        q = jnp.concatenate([q[..., :QK_NOPE_DIM], _rope_2d(q[..., QK_NOPE_DIM:], cos, sin)], axis=-1)
    return q * SOFTMAX_SCALE


def _conformer_conv(u_glu, p):
    a, g = jnp.split(u_glu, 2, axis=-1)
    u = a * jax.nn.sigmoid(g)
    u = lax.conv_general_dilated(u, p['dw_conv_w'][:, None, :], window_strides=(1,),
                                 padding=[(CONV_PAD, CONV_PAD)],
                                 dimension_numbers=('NWC', 'WIO', 'NWC'),
                                 feature_group_count=CONV_CHANNELS) + p['dw_conv_b']
    u = jax.nn.silu(_layer_norm(u, p['conv_norm_g'], p['conv_norm_b']))
    return u @ p['w_conv_out']


def _mixer(h, p, cos, sin, ctx_k, ctx_v):
    b, l, _ = h.shape
    z = h @ p['w_in']
    kv_lat, k_r, q_lat, u_glu, gate_logit = jnp.split(z, PROJ_SPLITS, axis=-1)
    k, v = _mla_keys_values(kv_lat, k_r, p, cos, sin)
    q = _mla_queries(q_lat, p, cos, sin)
    if ctx_k is None:
        k_all, v_all = k, v
    else:
        k_all = jnp.concatenate([ctx_k, k], axis=1)
        v_all = jnp.concatenate([ctx_v, v], axis=1)
    attn = _attention(q, k_all, v_all).reshape(b, l, MLA_HEADS * V_HEAD_DIM) @ p['w_mla_out']
    conv = _conformer_conv(u_glu, p)
    g_conv, g_mla = jnp.split(jax.nn.sigmoid(gate_logit), N_BRANCHES, axis=-1)
    out = (g_conv * conv + g_mla * attn) @ p['w_out']
    return out, k, v


def _context_kv(h_c, p):
    z = h_c @ p['w_in'][:, :PROJ_SPLITS[1]]
    kv_lat, k_r = jnp.split(z, [KV_LORA_RANK], axis=-1)
    return _mla_keys_values(kv_lat, k_r, p, None, None)


def _residual_and_mlp(x, mixer_out, mod, p):
    shift_a, scale_a, gate_a, shift_m, scale_m, gate_m = mod
    x = _layer_norm(DEEPNORM_ALPHA * x + gate_a * mixer_out, p['ln_mix_g'], p['ln_mix_b'])
    h = _modulate(x, shift_m, scale_m)
    f = jnp.square(jax.nn.relu(h @ p['w_mlp_up'])) @ p['w_mlp_down']
    return _layer_norm(DEEPNORM_ALPHA * x + gate_m * f, p['ln_mlp_g'], p['ln_mlp_b'])


def _normal(k, shape, scale):
    return jax.random.normal(k, shape, jnp.float32) * scale


def setup_inputs(seed: int = 0) -> dict:
    key = jax.random.key(seed)
    ks = jax.random.split(key, 26)
    L, D, C = DEPTH, D_MODEL, CONV_CHANNELS
    return {
        'x': _normal(ks[0], (BATCH, SEQ, D), 1.0),
        'c': _normal(ks[1], (BATCH, D), 1.0),
        'ctx': _normal(ks[2], (BATCH, CTX_LEN, D), 1.0),
        'c_ctx': _normal(ks[3], (D,), 1.0),
        'w_ada': _normal(ks[4], (L, D, N_MOD * D), 0.5 * D ** -0.5),
        'b_ada': _normal(ks[5], (L, N_MOD * D), 0.05),
        'w_in': _normal(ks[6], (L, D, PROJ_WIDTH), D ** -0.5),
        'dw_conv_w': _normal(ks[7], (L, CONV_WIDTH, C), CONV_WIDTH ** -0.5),
        'dw_conv_b': _normal(ks[8], (L, C), 0.02),
        'conv_norm_g': 1.0 + _normal(ks[9], (L, C), 0.02),
        'conv_norm_b': _normal(ks[10], (L, C), 0.02),
        'w_conv_out': _normal(ks[11], (L, C, D), DEEPNORM_BETA * C ** -0.5),
        'q_norm_g': 1.0 + _normal(ks[12], (L, Q_LORA_RANK), 0.02),
        'kv_norm_g': 1.0 + _normal(ks[13], (L, KV_LORA_RANK), 0.02),
        'w_uq': _normal(ks[14], (L, Q_LORA_RANK, MLA_HEADS * (QK_NOPE_DIM + QK_ROPE_DIM)), Q_LORA_RANK ** -0.5),
        'w_ukv': _normal(ks[15], (L, KV_LORA_RANK, MLA_HEADS * (QK_NOPE_DIM + V_HEAD_DIM)), KV_LORA_RANK ** -0.5),
        'w_mla_out': _normal(ks[16], (L, MLA_HEADS * V_HEAD_DIM, D), DEEPNORM_BETA * (MLA_HEADS * V_HEAD_DIM) ** -0.5),
        'w_out': _normal(ks[17], (L, D, D), DEEPNORM_BETA * D ** -0.5),
        'ln_mix_g': 1.0 + _normal(ks[18], (L, D), 0.02),
        'ln_mix_b': _normal(ks[19], (L, D), 0.02),
        'w_mlp_up': _normal(ks[20], (L, D, D_FF), DEEPNORM_BETA * D ** -0.5),
        'w_mlp_down': _normal(ks[21], (L, D_FF, D), DEEPNORM_BETA * D_FF ** -0.5),
        'ln_mlp_g': 1.0 + _normal(ks[22], (L, D), 0.02),
        'ln_mlp_b': _normal(ks[23], (L, D), 0.02),
    }


def reference(x, c, ctx, c_ctx, w_ada, b_ada, w_in, dw_conv_w, dw_conv_b, conv_norm_g, conv_norm_b,
              w_conv_out, q_norm_g, kv_norm_g, w_uq, w_ukv, w_mla_out, w_out, ln_mix_g, ln_mix_b,
              w_mlp_up, w_mlp_down, ln_mlp_g, ln_mlp_b):
    cos, sin = _grid_rope_tables(x.shape[1])
    for l in range(DEPTH):
        p = {'w_in': w_in[l], 'dw_conv_w': dw_conv_w[l], 'dw_conv_b': dw_conv_b[l],
             'conv_norm_g': conv_norm_g[l], 'conv_norm_b': conv_norm_b[l], 'w_conv_out': w_conv_out[l],
             'q_norm_g': q_norm_g[l], 'kv_norm_g': kv_norm_g[l], 'w_uq': w_uq[l], 'w_ukv': w_ukv[l],
             'w_mla_out': w_mla_out[l], 'w_out': w_out[l], 'ln_mix_g': ln_mix_g[l], 'ln_mix_b': ln_mix_b[l],
             'w_mlp_up': w_mlp_up[l], 'w_mlp_down': w_mlp_down[l], 'ln_mlp_g': ln_mlp_g[l],
             'ln_mlp_b': ln_mlp_b[l]}
        mod_x = jnp.split((jax.nn.silu(c) @ w_ada[l] + b_ada[l])[:, None, :], N_MOD, axis=-1)
        mod_c = jnp.split(jax.nn.silu(c_ctx) @ w_ada[l] + b_ada[l], N_MOD, axis=-1)
        h_c = _modulate(ctx, mod_c[0], mod_c[1])
        if l < DEPTH - 1:
            out_c, k_c, v_c = _mixer(h_c, p, None, None, None, None)
            ctx_next = _residual_and_mlp(ctx, out_c, mod_c, p)
        else:
            k_c, v_c = _context_kv(h_c, p)
            ctx_next = ctx
        h_x = _modulate(x, mod_x[0], mod_x[1])
        out_x, _, _ = _mixer(h_x, p, cos, sin, k_c, v_c)
        x = _residual_and_mlp(x, out_x, mod_x, p)
        ctx = ctx_next
    return x
```

```python
import functools

import jax
import jax.numpy as jnp
from jax import lax
from jax.experimental import pallas as pl
from jax.experimental.pallas import tpu as pltpu

F32 = jnp.float32
BF16 = jnp.bfloat16

GRID_W = 64
QK_NOPE_DIM = 128
QK_ROPE_DIM = 64
V_HEAD_DIM = 128
QK_PAD_DIM = 256
ROPE_THETA = 10000.0
ROPE_FREQS = QK_ROPE_DIM // 4
SOFTMAX_SCALE = (QK_NOPE_DIM + QK_ROPE_DIM) ** -0.5
CONV_WIDTH = 31
CONV_PAD = CONV_WIDTH // 2
CONV_HALO = 16
N_MOD = 6
LN_EPS = 1e-6
DEPTH = 1
DEEPNORM_ALPHA = (2 * DEPTH) ** 0.25
LANES = 128
V7X_VMEM_BYTES = 64 * 1024 * 1024


def _params(semantics, vmem_mb):
    return pltpu.CompilerParams(dimension_semantics=semantics,
                                vmem_limit_bytes=min(vmem_mb * 1024 * 1024, V7X_VMEM_BYTES - (4 << 20)))


def _resident(shape):
    nd = len(shape)
    return pl.BlockSpec(shape, lambda *_: (0,) * nd, pipeline_mode=pl.Buffered(1))


def _tile(n, want):
    t = min(n, want)
    assert n % t == 0, (n, t)
    return t


def _ln(x):
    xc = x - jnp.mean(x, axis=-1, keepdims=True)
    return xc * lax.rsqrt(jnp.mean(xc * xc, axis=-1, keepdims=True) + LN_EPS)


def _rms(x, gain):
    return x * lax.rsqrt(jnp.mean(x * x, axis=-1, keepdims=True) + LN_EPS) * gain


def _sigmoid(x):
    return 1.0 / (1.0 + jnp.exp(-x))


def _dot(a, b):
    return jnp.dot(a, b, preferred_element_type=F32)


def _ada_kernel(c_ref, w_ref, b_ref, o_ref):
    c = c_ref[...]
    a = (c * _sigmoid(c)).astype(BF16)
    o_ref[...] = _dot(a, w_ref[...].astype(BF16)) + b_ref[...]


def _ada(c_rows, w_ada, b_ada):
    m, d = c_rows.shape
    n = w_ada.shape[1]
    tn = _tile(n, 1536)
    return pl.pallas_call(
        _ada_kernel,
        grid=(n // tn,),
        in_specs=[pl.BlockSpec((m, d), lambda j: (0, 0)),
                  pl.BlockSpec((d, tn), lambda j: (0, j)),
                  pl.BlockSpec((1, tn), lambda j: (0, j))],
        out_specs=pl.BlockSpec((m, tn), lambda j: (0, j)),
        out_shape=jax.ShapeDtypeStruct((m, n), F32),
        compiler_params=_params(("arbitrary",), 40),
        name="ada",
    )(c_rows, w_ada, b_ada.reshape(1, n))


def _proj_in_kernel(*refs, kv_rank, full):
    if full:
        (x_ref, sh_ref, sc_ref, c2_ref, s2_ref, wkv_ref, kvg_ref, wq_ref, qg_ref, wglu_ref, wgate_ref,
         kvn_ref, kpe_ref, qn_ref, u_ref, gate_ref) = refs
    else:
        x_ref, sh_ref, sc_ref, c2_ref, s2_ref, wkv_ref, kvg_ref, kvn_ref, kpe_ref = refs
    h = (_ln(x_ref[0]) * (1.0 + sc_ref[0]) + sh_ref[0]).astype(BF16)
    zkv = _dot(h, wkv_ref[...])
    kvn_ref[0] = _rms(zkv[:, :kv_rank], kvg_ref[...]).astype(BF16)
    kpe = (zkv[:, kv_rank:kv_rank + LANES] * c2_ref[...]
           + zkv[:, kv_rank + LANES:kv_rank + 2 * LANES] * s2_ref[...])
    kpe_ref[0] = kpe.astype(BF16)
    if full:
        qn_ref[0] = _rms(_dot(h, wq_ref[...]), qg_ref[...]).astype(BF16)
        c = wglu_ref.shape[1] // 2
        u_ref[0] = _dot(h, wglu_ref[:, :c]) * _sigmoid(_dot(h, wglu_ref[:, c:]))
        ng = wgate_ref.shape[1]
        step = min(ng, 1024)
        for c0 in range(0, ng, step):
            gate_ref[0, :, c0:c0 + step] = _sigmoid(_dot(h, wgate_ref[:, c0:c0 + step])).astype(BF16)


def _proj_in(x, shift, scale, c2, s2, w_kv, kv_gain, w_q=None, q_gain=None, w_glu=None, w_gate=None):
    b, l, d = x.shape
    full = w_q is not None
    ts = _tile(l, 256)
    kv_rank = kv_gain.shape[-1]
    per_batch = shift.shape[0] == b and b > 1
    mod_spec = pl.BlockSpec((1, 1, d), (lambda i, j: (i, 0, 0)) if per_batch else (lambda i, j: (0, 0, 0)))
    row = lambda n: pl.BlockSpec((1, ts, n), lambda i, j: (i, j, 0))
    tab = pl.BlockSpec((ts, LANES), lambda i, j: (j, 0))
    in_specs = [row(d), mod_spec, mod_spec, tab, tab, _resident(w_kv.shape), _resident((1, kv_rank))]
    args = [x, shift, scale, c2, s2, w_kv, kv_gain.reshape(1, kv_rank)]
    out_specs = [row(kv_rank), row(LANES)]
    out_shape = [jax.ShapeDtypeStruct((b, l, kv_rank), BF16), jax.ShapeDtypeStruct((b, l, LANES), BF16)]
    if full:
        q_rank = q_gain.shape[-1]
        c = w_glu.shape[1] // 2
        ng = w_gate.shape[1]
        in_specs += [_resident(w_q.shape), _resident((1, q_rank)), _resident(w_glu.shape), _resident(w_gate.shape)]
        args += [w_q, q_gain.reshape(1, q_rank), w_glu, w_gate]
        out_specs += [row(q_rank), row(c), row(ng)]
        out_shape += [jax.ShapeDtypeStruct((b, l, q_rank), BF16), jax.ShapeDtypeStruct((b, l, c), F32),
                      jax.ShapeDtypeStruct((b, l, ng), BF16)]
    return pl.pallas_call(
        functools.partial(_proj_in_kernel, kv_rank=kv_rank, full=full),
        grid=(b, l // ts),
        in_specs=in_specs, out_specs=out_specs, out_shape=out_shape,
        compiler_params=_params(("arbitrary", "arbitrary"), 56),
        name="proj_in" if full else "proj_in_ctx",
    )(*args)


def _kv_up_kernel(kvn_c_ref, kpe_c_ref, kvn_x_ref, kpe_x_ref, wk_ref, wv_ref, k_ref, v_ref, *, n_ctx_tiles, heads):
    is_ctx = pl.program_id(1) < n_ctx_tiles

    def emit(kvn, kpe):
        lane = lax.broadcasted_iota(jnp.int32, kpe.shape, 1)
        zero = jnp.zeros_like(kpe)
        kpe_even = jnp.where(lane < QK_ROPE_DIM, kpe, zero)
        kpe_odd = jnp.where(lane >= QK_ROPE_DIM, kpe, zero)
        kn = _dot(kvn, wk_ref[...]).astype(BF16)
        vv = _dot(kvn, wv_ref[...]).astype(BF16)
        for h in range(heads):
            k_ref[0, h, :, :QK_NOPE_DIM] = kn[:, h * QK_NOPE_DIM:(h + 1) * QK_NOPE_DIM]
            k_ref[0, h, :, QK_NOPE_DIM:] = kpe_even if h % 2 == 0 else kpe_odd
            v_ref[0, h] = vv[:, h * V_HEAD_DIM:(h + 1) * V_HEAD_DIM]

    @pl.when(is_ctx)
    def _():
        emit(kvn_c_ref[0], kpe_c_ref[0])

    @pl.when(jnp.logical_not(is_ctx))
    def _():
        emit(kvn_x_ref[0], kpe_x_ref[0])


def _kv_up(kvn_c, kpe_c, kvn_x, kpe_x, w_k, w_v, heads):
    b, lc, r = kvn_c.shape
    lx = kvn_x.shape[1]
    tl = _tile(lc, 256)
    assert lx % tl == 0
    nct = lc // tl
    lk = lc + lx
    ctx_map = lambda i, j: (i, jnp.minimum(j, nct - 1), 0)
    x_map = lambda i, j: (i, jnp.maximum(j - nct, 0), 0)
    return pl.pallas_call(
        functools.partial(_kv_up_kernel, n_ctx_tiles=nct, heads=heads),
        grid=(b, lk // tl),
        in_specs=[pl.BlockSpec((1, tl, r), ctx_map), pl.BlockSpec((1, tl, LANES), ctx_map),
                  pl.BlockSpec((1, tl, r), x_map), pl.BlockSpec((1, tl, LANES), x_map),
                  _resident(w_k.shape), _resident(w_v.shape)],
        out_specs=[pl.BlockSpec((1, heads, tl, QK_PAD_DIM), lambda i, j: (i, 0, j, 0)),
                   pl.BlockSpec((1, heads, tl, V_HEAD_DIM), lambda i, j: (i, 0, j, 0))],
        out_shape=[jax.ShapeDtypeStruct((b, heads, lk, QK_PAD_DIM), BF16),
                   jax.ShapeDtypeStruct((b, heads, lk, V_HEAD_DIM), BF16)],
        compiler_params=_params(("arbitrary", "arbitrary"), 40),
        name="kv_up",
    )(kvn_c, kpe_c, kvn_x, kpe_x, w_k, w_v)


def _q_up_kernel(qn_ref, c2_ref, s2_ref, wn_ref, wr_ref, ws_ref, q_ref, *, heads):
    qn = qn_ref[0]
    nope = _dot(qn, wn_ref[...])
    rope = _dot(qn, wr_ref[...])
    swap = _dot(qn, ws_ref[...])
    c2 = c2_ref[...]
    s2 = s2_ref[...]
    lane = lax.broadcasted_iota(jnp.int32, c2.shape, 1)
    for p in range(heads // 2):
        rot = (rope[:, p * LANES:(p + 1) * LANES] * c2 + swap[:, p * LANES:(p + 1) * LANES] * s2) * SOFTMAX_SCALE
        rot_even = jnp.where(lane < QK_ROPE_DIM, rot, 0.0).astype(BF16)
        rot_odd = jnp.where(lane >= QK_ROPE_DIM, rot, 0.0).astype(BF16)
        for h, rot_h in ((2 * p, rot_even), (2 * p + 1, rot_odd)):
            q_ref[0, h, :, :QK_NOPE_DIM] = (nope[:, h * QK_NOPE_DIM:(h + 1) * QK_NOPE_DIM] * SOFTMAX_SCALE).astype(BF16)
            q_ref[0, h, :, QK_NOPE_DIM:] = rot_h


def _q_up(qn, c2, s2, w_nope, w_rope, w_swap, heads):
    b, l, r = qn.shape
    tl = _tile(l, 256)
    tab = pl.BlockSpec((tl, LANES), lambda i, j: (j, 0))
    return pl.pallas_call(
        functools.partial(_q_up_kernel, heads=heads),
        grid=(b, l // tl),
        in_specs=[pl.BlockSpec((1, tl, r), lambda i, j: (i, j, 0)), tab, tab,
                  _resident(w_nope.shape), _resident(w_rope.shape), _resident(w_swap.shape)],
        out_specs=pl.BlockSpec((1, heads, tl, QK_PAD_DIM), lambda i, j: (i, 0, j, 0)),
        out_shape=jax.ShapeDtypeStruct((b, heads, l, QK_PAD_DIM), BF16),
        compiler_params=_params(("arbitrary", "arbitrary"), 40),
        name="q_up",
    )(qn, c2, s2, w_nope, w_rope, w_swap)


def _attention_kernel(q_ref, k_ref, v_ref, o_ref, *, tq):
    n_tiles = q_ref.shape[2] // tq

    def body(i, carry):
        r0 = pl.multiple_of(i * tq, tq)
        q = q_ref[0, 0, pl.ds(r0, tq), :]
        s = lax.dot_general(q, k_ref[0, 0], (((1,), (1,)), ((), ())), preferred_element_type=F32)
        p = jnp.exp(s - jnp.max(s, axis=-1, keepdims=True))
        denom = jnp.sum(p, axis=-1, keepdims=True)
        o = _dot(p.astype(BF16), v_ref[0, 0]) / denom
        o_ref[0, pl.ds(r0, tq), :] = o.astype(BF16)
        return carry

    lax.fori_loop(0, n_tiles, body, 0)


def _attention(q, k, v):
    b, heads, s, _ = q.shape
    lk = k.shape[2]
    tq = _tile(s, 256)
    return pl.pallas_call(
        functools.partial(_attention_kernel, tq=tq),
        grid=(b, heads),
        in_specs=[pl.BlockSpec((1, 1, s, QK_PAD_DIM), lambda i, h: (i, h, 0, 0)),
                  pl.BlockSpec((1, 1, lk, QK_PAD_DIM), lambda i, h: (i, h, 0, 0)),
                  pl.BlockSpec((1, 1, lk, V_HEAD_DIM), lambda i, h: (i, h, 0, 0))],
        out_specs=pl.BlockSpec((1, s, V_HEAD_DIM), lambda i, h: (i, 0, h)),
        out_shape=jax.ShapeDtypeStruct((b, s, heads * V_HEAD_DIM), BF16),
        compiler_params=_params(("arbitrary", "arbitrary"), 48),
        name="attention",
    )(q, k, v)


def _conv_kernel(u_ref, prev_ref, next_ref, dw_ref, db_ref, g_ref, b_ref, wo_ref, o_ref, ubuf, cbuf, *, ts):
    j = pl.program_id(1)
    last = pl.num_programs(1) - 1
    c = ubuf.shape[1]
    halo_zeros = jnp.zeros((CONV_HALO, c), F32)
    ubuf[CONV_HALO:CONV_HALO + ts, :] = u_ref[0]

    @pl.when(j > 0)
    def _():
        ubuf[0:CONV_HALO, :] = prev_ref[0]

    @pl.when(j == 0)
    def _():
        ubuf[0:CONV_HALO, :] = halo_zeros

    @pl.when(j < last)
    def _():
        ubuf[CONV_HALO + ts:, :] = next_ref[0]

    @pl.when(j == last)
    def _():
        ubuf[CONV_HALO + ts:, :] = halo_zeros

    rows = min(ts, 64)
    cols = min(c, 256)
    base = CONV_HALO - CONV_PAD
    for r0 in range(0, ts, rows):
        for c0 in range(0, c, cols):
            acc = jnp.broadcast_to(db_ref[:, c0:c0 + cols], (rows, cols))
            for t in range(CONV_WIDTH):
                acc = acc + ubuf[r0 + base + t:r0 + base + t + rows, c0:c0 + cols] * dw_ref[t:t + 1, c0:c0 + cols]
            cbuf[r0:r0 + rows, c0:c0 + cols] = acc
    y = _ln(cbuf[...]) * g_ref[...] + b_ref[...]
    y = (y * _sigmoid(y)).astype(BF16)
    o_ref[0] = _dot(y, wo_ref[...])


def _conv_branch(u, dw_w, dw_b, g, bb, w_out):
    b, s, c = u.shape
    d = w_out.shape[1]
    ts = _tile(s, 256)
    hb = ts // CONV_HALO
    n_halo = s // CONV_HALO
    return pl.pallas_call(
        functools.partial(_conv_kernel, ts=ts),
        grid=(b, s // ts),
        in_specs=[pl.BlockSpec((1, ts, c), lambda i, j: (i, j, 0)),
                  pl.BlockSpec((1, CONV_HALO, c), lambda i, j: (i, jnp.maximum(j * hb - 1, 0), 0)),
                  pl.BlockSpec((1, CONV_HALO, c), lambda i, j: (i, jnp.minimum((j + 1) * hb, n_halo - 1), 0)),
                  _resident((CONV_WIDTH, c)), _resident((1, c)), _resident((1, c)), _resident((1, c)),
                  _resident(w_out.shape)],
        out_specs=pl.BlockSpec((1, ts, d), lambda i, j: (i, j, 0)),
        out_shape=jax.ShapeDtypeStruct((b, s, d), F32),
        scratch_shapes=[pltpu.VMEM((ts + 2 * CONV_HALO, c), F32), pltpu.VMEM((ts, c), F32)],
        compiler_params=_params(("arbitrary", "arbitrary"), 40),
        name="conv_branch",
    )(u, u, u, dw_w, dw_b.reshape(1, c), g.reshape(1, c), bb.reshape(1, c), w_out)


def _merge_kernel(attn_ref, conv_ref, gate_ref, x_ref, ga_ref, g_ref, b_ref, wmla_ref, wout_ref, o_ref):
    d = x_ref.shape[2]
    attn = _dot(attn_ref[0], wmla_ref[...])
    merged = gate_ref[0, :, :d].astype(F32) * conv_ref[0] + gate_ref[0, :, d:].astype(F32) * attn
    out = _dot(merged.astype(BF16), wout_ref[...])
    y = DEEPNORM_ALPHA * x_ref[0] + ga_ref[0] * out
    o_ref[0] = _ln(y) * g_ref[...] + b_ref[...]


def _merge(attn, conv, gates, x, gate_a, g, bb, w_mla, w_out):
    b, s, d = x.shape
    ts = _tile(s, 256)
    row = lambda n: pl.BlockSpec((1, ts, n), lambda i, j: (i, j, 0))
    return pl.pallas_call(
        _merge_kernel,
        grid=(b, s // ts),
        in_specs=[row(attn.shape[2]), row(d), row(2 * d), row(d),
                  pl.BlockSpec((1, 1, d), lambda i, j: (i, 0, 0)),
                  _resident((1, d)), _resident((1, d)), _resident(w_mla.shape), _resident(w_out.shape)],
        out_specs=row(d),
        out_shape=jax.ShapeDtypeStruct((b, s, d), F32),
        compiler_params=_params(("arbitrary", "arbitrary"), 48),
        name="merge",
    )(attn, conv, gates, x, gate_a, g.reshape(1, d), bb.reshape(1, d), w_mla, w_out)


def _mlp_kernel(x_ref, sh_ref, sc_ref, gm_ref, g_ref, b_ref, wu_ref, wd_ref, o_ref, h_scr, acc_scr):
    k = pl.program_id(2)

    @pl.when(k == 0)
    def _():
        h_scr[...] = (_ln(x_ref[0]) * (1.0 + sc_ref[0]) + sh_ref[0]).astype(BF16)
        acc_scr[...] = jnp.zeros_like(acc_scr)

    a = jnp.maximum(_dot(h_scr[...], wu_ref[...]), 0.0)
    acc_scr[...] += _dot((a * a).astype(BF16), wd_ref[...])

    @pl.when(k == pl.num_programs(2) - 1)
    def _():
        y = DEEPNORM_ALPHA * x_ref[0] + gm_ref[0] * acc_scr[...]
        o_ref[0] = _ln(y) * g_ref[...] + b_ref[...]


def _mlp(x, shift, scale, gate_m, g, bb, w_up, w_down):
    b, s, d = x.shape
    dff = w_up.shape[1]
    tm = _tile(s, 512)
    tk = _tile(dff, 512)
    row = pl.BlockSpec((1, tm, d), lambda i, j, k: (i, j, 0))
    mod = pl.BlockSpec((1, 1, d), lambda i, j, k: (i, 0, 0))
    vec = pl.BlockSpec((1, d), lambda i, j, k: (0, 0))
    return pl.pallas_call(
        _mlp_kernel,
        grid=(b, s // tm, dff // tk),
        in_specs=[row, mod, mod, mod, vec, vec,
                  pl.BlockSpec((d, tk), lambda i, j, k: (0, k)),
                  pl.BlockSpec((tk, d), lambda i, j, k: (k, 0))],
        out_specs=row,
        out_shape=jax.ShapeDtypeStruct((b, s, d), F32),
        scratch_shapes=[pltpu.VMEM((tm, d), BF16), pltpu.VMEM((tm, d), F32)],
        compiler_params=_params(("arbitrary", "arbitrary", "arbitrary"), 48),
        name="mlp",
    )(x, shift, scale, gate_m, g.reshape(1, d), bb.reshape(1, d), w_up, w_down)


def _rope_tables(n_tok):
    rows = n_tok // GRID_W
    row = jnp.repeat(jnp.arange(rows, dtype=F32), GRID_W)
    col = jnp.tile(jnp.arange(GRID_W, dtype=F32), rows)
    inv_freq = ROPE_THETA ** (-jnp.arange(ROPE_FREQS, dtype=F32) / ROPE_FREQS)
    ang = jnp.stack([row[:, None] * inv_freq, col[:, None] * inv_freq], axis=1)
    cos, sin = jnp.cos(ang), jnp.sin(ang)
    c = jnp.stack([cos, cos], axis=2).reshape(n_tok, QK_ROPE_DIM)
    s = jnp.stack([-sin, sin], axis=2).reshape(n_tok, QK_ROPE_DIM)
    return jnp.concatenate([c, c], axis=1), jnp.concatenate([s, s], axis=1)


def _swap_halves(w):
    lead = w.shape[:-1]
    return w.reshape(lead + (2, 2, ROPE_FREQS))[..., ::-1, :].reshape(lead + (QK_ROPE_DIM,))


def kernel(x, c, ctx, c_ctx, w_ada, b_ada, w_in, dw_conv_w, dw_conv_b, conv_norm_g, conv_norm_b, w_conv_out, q_norm_g, kv_norm_g, w_uq, w_ukv, w_mla_out, w_out, ln_mix_g, ln_mix_b, w_mlp_up, w_mlp_down, ln_mlp_g, ln_mlp_b):
    b, s, d = x.shape
    lc = ctx.shape[1]
    kv_rank = kv_norm_g.shape[-1]
    q_rank = q_norm_g.shape[-1]
    conv_c = dw_conv_w.shape[-1]
    heads = w_uq.shape[-1] // (QK_NOPE_DIM + QK_ROPE_DIM)
    assert w_ada.shape[0] == DEPTH and heads % 2 == 0

    n_rows = -(-(b + 1) // 8) * 8
    c_rows = jnp.zeros((n_rows, d), F32).at[:b].set(c).at[b].set(c_ctx)
    mod = _ada(c_rows, w_ada[0], b_ada[0])
    mod_x = [mod[:b, i * d:(i + 1) * d].reshape(b, 1, d) for i in range(N_MOD)]
    mod_c = [mod[b:b + 1, i * d:(i + 1) * d].reshape(1, 1, d) for i in range(2)]

    wi = w_in[0]
    o_kr, o_q, o_glu, o_gate = kv_rank, kv_rank + QK_ROPE_DIM, kv_rank + QK_ROPE_DIM + q_rank, kv_rank + QK_ROPE_DIM + q_rank + 2 * conv_c
    w_kr = wi[:, o_kr:o_q]
    w_kr_sw = _swap_halves(w_kr)
    w_kv = jnp.concatenate([wi[:, :o_kr], w_kr, w_kr, w_kr_sw, w_kr_sw], axis=1).astype(BF16)
    w_q = wi[:, o_q:o_glu].astype(BF16)
    w_glu = wi[:, o_glu:o_gate].astype(BF16)
    w_gate = wi[:, o_gate:].astype(BF16)

    c2, s2 = _rope_tables(s)
    ones = jnp.ones((lc, LANES), F32)
    kvn_c, kpe_c = _proj_in(ctx, mod_c[0], mod_c[1], ones, jnp.zeros_like(ones), w_kv, kv_norm_g[0])
    kvn_x, kpe_x, qn, u, gates = _proj_in(x, mod_x[0], mod_x[1], c2, s2, w_kv, kv_norm_g[0],
                                          w_q, q_norm_g[0], w_glu, w_gate)

    wkv3 = w_ukv[0].reshape(kv_rank, heads, QK_NOPE_DIM + V_HEAD_DIM)
    w_k = wkv3[:, :, :QK_NOPE_DIM].reshape(kv_rank, heads * QK_NOPE_DIM).astype(BF16)
    w_v = wkv3[:, :, QK_NOPE_DIM:].reshape(kv_rank, heads * V_HEAD_DIM).astype(BF16)
    k_all, v_all = _kv_up(kvn_c, kpe_c, kvn_x, kpe_x, w_k, w_v, heads)

    wq3 = w_uq[0].reshape(q_rank, heads, QK_NOPE_DIM + QK_ROPE_DIM)
    w_qn = wq3[:, :, :QK_NOPE_DIM].reshape(q_rank, heads * QK_NOPE_DIM).astype(BF16)
    w_qr = wq3[:, :, QK_NOPE_DIM:]
    w_qs = _swap_halves(w_qr).reshape(q_rank, heads * QK_ROPE_DIM).astype(BF16)
    w_qr = w_qr.reshape(q_rank, heads * QK_ROPE_DIM).astype(BF16)
    q = _q_up(qn, c2, s2, w_qn, w_qr, w_qs, heads)

    attn = _attention(q, k_all, v_all)
    conv = _conv_branch(u, dw_conv_w[0], dw_conv_b[0], conv_norm_g[0], conv_norm_b[0], w_conv_out[0].astype(BF16))
    x1 = _merge(attn, conv, gates, x, mod_x[2], ln_mix_g[0], ln_mix_b[0],
                w_mla_out[0].astype(BF16), w_out[0].astype(BF16))
    return _mlp(x1, mod_x[3], mod_x[4], mod_x[5], ln_mlp_g[0], ln_mlp_b[0],
                w_mlp_up[0].astype(BF16), w_mlp_down[0].astype(BF16))
```

```python
import functools

import jax
import jax.numpy as jnp
from jax import lax
from jax.experimental import pallas as pl
from jax.experimental.pallas import tpu as pltpu

F32 = jnp.float32
BF16 = jnp.bfloat16

GRID_W = 64
QK_NOPE_DIM = 128
QK_ROPE_DIM = 64
V_HEAD_DIM = 128
QK_PAD_DIM = 256
ROPE_THETA = 10000.0
ROPE_FREQS = QK_ROPE_DIM // 4
SOFTMAX_SCALE = (QK_NOPE_DIM + QK_ROPE_DIM) ** -0.5
LOG2_E = 1.4426950408889634
Q_SCALE = SOFTMAX_SCALE * LOG2_E
CONV_WIDTH = 31
CONV_PAD = CONV_WIDTH // 2
CONV_HALO = 16
N_MOD = 6
LN_EPS = 1e-6
DEPTH = 1
DEEPNORM_ALPHA = (2 * DEPTH) ** 0.25
LANES = 128
SUBLANES = 8
V7X_VMEM_BYTES = 64 * 1024 * 1024


def _params(semantics, vmem_mb, flags=None):
    return pltpu.CompilerParams(dimension_semantics=semantics, flags=flags,
                                vmem_limit_bytes=min(vmem_mb * 1024 * 1024, V7X_VMEM_BYTES - (4 << 20)))


def _resident(shape):
    nd = len(shape)
    return pl.BlockSpec(shape, lambda *_: (0,) * nd, pipeline_mode=pl.Buffered(1))


def _tile(n, want):
    t = min(n, want)
    assert n % t == 0, (n, t)
    return t


def _ln(x):
    xc = x - jnp.mean(x, axis=-1, keepdims=True)
    return xc * lax.rsqrt(jnp.mean(xc * xc, axis=-1, keepdims=True) + LN_EPS)


def _rms(x, gain):
    return x * lax.rsqrt(jnp.mean(x * x, axis=-1, keepdims=True) + LN_EPS) * gain


def _sigmoid(x):
    return 1.0 / (1.0 + jnp.exp(-x))


def _dot(a, b):
    return jnp.dot(a, b, preferred_element_type=F32)


def _ada_kernel(c_ref, w_ref, b_ref, o_ref):
    c = c_ref[...]
    a = (c * _sigmoid(c)).astype(BF16)
    o_ref[...] = _dot(a, w_ref[...].astype(BF16)) + b_ref[...]


def _ada(c_rows, w_ada, b_ada):
    m, d = c_rows.shape
    n = w_ada.shape[1]
    tn = _tile(n, 1536)
    return pl.pallas_call(
        _ada_kernel,
        grid=(n // tn,),
        in_specs=[pl.BlockSpec((m, d), lambda j: (0, 0)),
                  pl.BlockSpec((d, tn), lambda j: (0, j)),
                  pl.BlockSpec((1, tn), lambda j: (0, j))],
        out_specs=pl.BlockSpec((m, tn), lambda j: (0, j)),
        out_shape=jax.ShapeDtypeStruct((m, n), F32),
        compiler_params=_params(("arbitrary",), 40),
        name="ada",
    )(c_rows, w_ada, b_ada.reshape(1, n))


def _proj_in_kernel(*refs, kv_rank, full):
    if full:
        (x_ref, sh_ref, sc_ref, c2_ref, s2_ref, wkv_ref, kvg_ref, wq_ref, qg_ref, wglu_ref, wgate_ref,
         kvn_ref, kpe_ref, qn_ref, u_ref, gate_ref) = refs
    else:
        x_ref, sh_ref, sc_ref, c2_ref, s2_ref, wkv_ref, kvg_ref, kvn_ref, kpe_ref = refs
    h = (_ln(x_ref[0]) * (1.0 + sc_ref[0]) + sh_ref[0]).astype(BF16)
    zkv = _dot(h, wkv_ref[...])
    kvn_ref[0] = _rms(zkv[:, :kv_rank], kvg_ref[...]).astype(BF16)
    kpe = (zkv[:, kv_rank:kv_rank + LANES] * c2_ref[...]
           + zkv[:, kv_rank + LANES:kv_rank + 2 * LANES] * s2_ref[...])
    kpe_ref[0] = kpe.astype(BF16)
    if full:
        qn_ref[0] = _rms(_dot(h, wq_ref[...]), qg_ref[...]).astype(BF16)
        c = wglu_ref.shape[1] // 2
        u_ref[0] = _dot(h, wglu_ref[:, :c]) * _sigmoid(_dot(h, wglu_ref[:, c:]))
        ng = wgate_ref.shape[1]
        step = min(ng, 1024)
        for c0 in range(0, ng, step):
            gate_ref[0, :, c0:c0 + step] = _sigmoid(_dot(h, wgate_ref[:, c0:c0 + step])).astype(BF16)


def _proj_in(x, shift, scale, c2, s2, w_kv, kv_gain, w_q=None, q_gain=None, w_glu=None, w_gate=None):
    b, l, d = x.shape
    full = w_q is not None
    ts = _tile(l, 256)
    kv_rank = kv_gain.shape[-1]
    per_batch = shift.shape[0] == b and b > 1
    mod_spec = pl.BlockSpec((1, 1, d), (lambda i, j: (i, 0, 0)) if per_batch else (lambda i, j: (0, 0, 0)))
    row = lambda n: pl.BlockSpec((1, ts, n), lambda i, j: (i, j, 0))
    tab = pl.BlockSpec((ts, LANES), lambda i, j: (j, 0))
    in_specs = [row(d), mod_spec, mod_spec, tab, tab, _resident(w_kv.shape), _resident((1, kv_rank))]
    args = [x, shift, scale, c2, s2, w_kv, kv_gain.reshape(1, kv_rank)]
    out_specs = [row(kv_rank), row(LANES)]
    out_shape = [jax.ShapeDtypeStruct((b, l, kv_rank), BF16), jax.ShapeDtypeStruct((b, l, LANES), BF16)]
    if full:
        q_rank = q_gain.shape[-1]
        c = w_glu.shape[1] // 2
        ng = w_gate.shape[1]
        in_specs += [_resident(w_q.shape), _resident((1, q_rank)), _resident(w_glu.shape), _resident(w_gate.shape)]
        args += [w_q, q_gain.reshape(1, q_rank), w_glu, w_gate]
        out_specs += [row(q_rank), row(c), row(ng)]
        out_shape += [jax.ShapeDtypeStruct((b, l, q_rank), BF16), jax.ShapeDtypeStruct((b, l, c), F32),
                      jax.ShapeDtypeStruct((b, l, ng), BF16)]
    return pl.pallas_call(
        functools.partial(_proj_in_kernel, kv_rank=kv_rank, full=full),
        grid=(b, l // ts),
        in_specs=in_specs, out_specs=out_specs, out_shape=out_shape,
        compiler_params=_params(("arbitrary", "arbitrary"), 56),
        name="proj_in" if full else "proj_in_ctx",
    )(*args)


def _kv_up_kernel(kvn_c_ref, kpe_c_ref, kvn_x_ref, kpe_x_ref, wk_ref, wv_ref, k_ref, v_ref, *, n_ctx_tiles, heads):
    is_ctx = pl.program_id(1) < n_ctx_tiles

    def emit(kvn, kpe):
        lane = lax.broadcasted_iota(jnp.int32, kpe.shape, 1)
        zero = jnp.zeros_like(kpe)
        kpe_even = jnp.where(lane < QK_ROPE_DIM, kpe, zero)
        kpe_odd = jnp.where(lane >= QK_ROPE_DIM, kpe, zero)
        kn = _dot(kvn, wk_ref[...]).astype(BF16)
        vv = _dot(kvn, wv_ref[...]).astype(BF16)
        for h in range(heads):
            k_ref[0, h, :, :QK_NOPE_DIM] = kn[:, h * QK_NOPE_DIM:(h + 1) * QK_NOPE_DIM]
            k_ref[0, h, :, QK_NOPE_DIM:] = kpe_even if h % 2 == 0 else kpe_odd
            v_ref[0, h] = vv[:, h * V_HEAD_DIM:(h + 1) * V_HEAD_DIM]

    @pl.when(is_ctx)
    def _():
        emit(kvn_c_ref[0], kpe_c_ref[0])

    @pl.when(jnp.logical_not(is_ctx))
    def _():
        emit(kvn_x_ref[0], kpe_x_ref[0])


def _kv_up(kvn_c, kpe_c, kvn_x, kpe_x, w_k, w_v, heads):
    b, lc, r = kvn_c.shape
    lx = kvn_x.shape[1]
    tl = _tile(lc, 256)
    assert lx % tl == 0
    nct = lc // tl
    lk = lc + lx
    ctx_map = lambda i, j: (i, jnp.minimum(j, nct - 1), 0)
    x_map = lambda i, j: (i, jnp.maximum(j - nct, 0), 0)
    return pl.pallas_call(
        functools.partial(_kv_up_kernel, n_ctx_tiles=nct, heads=heads),
        grid=(b, lk // tl),
        in_specs=[pl.BlockSpec((1, tl, r), ctx_map), pl.BlockSpec((1, tl, LANES), ctx_map),
                  pl.BlockSpec((1, tl, r), x_map), pl.BlockSpec((1, tl, LANES), x_map),
                  _resident(w_k.shape), _resident(w_v.shape)],
        out_specs=[pl.BlockSpec((1, heads, tl, QK_PAD_DIM), lambda i, j: (i, 0, j, 0)),
                   pl.BlockSpec((1, heads, tl, V_HEAD_DIM), lambda i, j: (i, 0, j, 0))],
        out_shape=[jax.ShapeDtypeStruct((b, heads, lk, QK_PAD_DIM), BF16),
                   jax.ShapeDtypeStruct((b, heads, lk, V_HEAD_DIM), BF16)],
        compiler_params=_params(("arbitrary", "arbitrary"), 40),
        name="kv_up",
    )(kvn_c, kpe_c, kvn_x, kpe_x, w_k, w_v)


def _q_up_kernel(qn_ref, c2_ref, s2_ref, wn_ref, wr_ref, ws_ref, q_ref, *, heads):
    qn = qn_ref[0]
    nope = _dot(qn, wn_ref[...])
    rope = _dot(qn, wr_ref[...])
    swap = _dot(qn, ws_ref[...])
    c2 = c2_ref[...]
    s2 = s2_ref[...]
    lane = lax.broadcasted_iota(jnp.int32, c2.shape, 1)
    for p in range(heads // 2):
        rot = (rope[:, p * LANES:(p + 1) * LANES] * c2 + swap[:, p * LANES:(p + 1) * LANES] * s2) * Q_SCALE
        rot_even = jnp.where(lane < QK_ROPE_DIM, rot, 0.0).astype(BF16)
        rot_odd = jnp.where(lane >= QK_ROPE_DIM, rot, 0.0).astype(BF16)
        for h, rot_h in ((2 * p, rot_even), (2 * p + 1, rot_odd)):
            q_ref[0, h, :, :QK_NOPE_DIM] = (nope[:, h * QK_NOPE_DIM:(h + 1) * QK_NOPE_DIM] * Q_SCALE).astype(BF16)
            q_ref[0, h, :, QK_NOPE_DIM:] = rot_h


def _q_up(qn, c2, s2, w_nope, w_rope, w_swap, heads):
    b, l, r = qn.shape
    tl = _tile(l, 256)
    tab = pl.BlockSpec((tl, LANES), lambda i, j: (j, 0))
    return pl.pallas_call(
        functools.partial(_q_up_kernel, heads=heads),
        grid=(b, l // tl),
        in_specs=[pl.BlockSpec((1, tl, r), lambda i, j: (i, j, 0)), tab, tab,
                  _resident(w_nope.shape), _resident(w_rope.shape), _resident(w_swap.shape)],
        out_specs=pl.BlockSpec((1, heads, tl, QK_PAD_DIM), lambda i, j: (i, 0, j, 0)),
        out_shape=jax.ShapeDtypeStruct((b, heads, l, QK_PAD_DIM), BF16),
        compiler_params=_params(("arbitrary", "arbitrary"), 40),
        name="q_up",
    )(qn, c2, s2, w_nope, w_rope, w_swap)


def _attention_kernel(q_ref, k_ref, v_ref, o_ref, *, tq):
    n_tiles = q_ref.shape[2] // tq

    def body(i, carry):
        r0 = pl.multiple_of(i * tq, tq)
        q = q_ref[0, 0, pl.ds(r0, tq), :]
        s = lax.dot_general(q, k_ref[0, 0], (((1,), (1,)), ((), ())), preferred_element_type=F32)
        p = jnp.exp2(s - jnp.max(s, axis=-1, keepdims=True))
        denom = jnp.sum(p, axis=-1, keepdims=True)
        o = _dot(p.astype(BF16), v_ref[0, 0]) / denom
        o_ref[0, pl.ds(r0, tq), :] = o.astype(BF16)
        return carry

    lax.fori_loop(0, n_tiles, body, 0, unroll=2)


def _attention(q, k, v):
    b, heads, s, _ = q.shape
    lk = k.shape[2]
    tq = _tile(s, 256)
    return pl.pallas_call(
        functools.partial(_attention_kernel, tq=tq),
        grid=(b, heads),
        in_specs=[pl.BlockSpec((1, 1, s, QK_PAD_DIM), lambda i, h: (i, h, 0, 0)),
                  pl.BlockSpec((1, 1, lk, QK_PAD_DIM), lambda i, h: (i, h, 0, 0)),
                  pl.BlockSpec((1, 1, lk, V_HEAD_DIM), lambda i, h: (i, h, 0, 0))],
        out_specs=pl.BlockSpec((1, s, V_HEAD_DIM), lambda i, h: (i, 0, h)),
        out_shape=jax.ShapeDtypeStruct((b, s, heads * V_HEAD_DIM), BF16),
        compiler_params=_params(("arbitrary", "arbitrary"), 48),
        name="attention",
    )(q, k, v)


def _conv_kernel(u_ref, prev_ref, next_ref, dw_ref, db_ref, g_ref, b_ref, wo_ref, o_ref, ubuf, cbuf, *, ts):
    j = pl.program_id(1)
    last = pl.num_programs(1) - 1
    c = ubuf.shape[1]
    halo_zeros = jnp.zeros((CONV_HALO, c), F32)
    ubuf[CONV_HALO:CONV_HALO + ts, :] = u_ref[0]

    @pl.when(j > 0)
    def _():
        ubuf[0:CONV_HALO, :] = prev_ref[0]

    @pl.when(j == 0)
    def _():
        ubuf[0:CONV_HALO, :] = halo_zeros

    @pl.when(j < last)
    def _():
        ubuf[CONV_HALO + ts:, :] = next_ref[0]

    @pl.when(j == last)
    def _():
        ubuf[CONV_HALO + ts:, :] = halo_zeros

    rows = min(ts, 64)
    cols = min(c, 128)
    base = CONV_HALO - CONV_PAD
    for r0 in range(0, ts, rows):
        for c0 in range(0, c, cols):
            acc = jnp.broadcast_to(db_ref[:, c0:c0 + cols], (rows, cols))
            for r in range(SUBLANES):
                n = rows if r == 0 else rows + SUBLANES
                part = None
                for t in range(CONV_WIDTH):
                    if (base + t) % SUBLANES != r:
                        continue
                    a = r0 + base + t - r
                    term = ubuf[a:a + n, c0:c0 + cols] * dw_ref[t:t + 1, c0:c0 + cols]
                    part = term if part is None else part + term
                if r:
                    part = pltpu.roll(part, n - r, axis=0)[:rows]
                acc = acc + part
            cbuf[r0:r0 + rows, c0:c0 + cols] = acc
    y = _ln(cbuf[...]) * g_ref[...] + b_ref[...]
    y = (y * _sigmoid(y)).astype(BF16)
    o_ref[0] = _dot(y, wo_ref[...])


def _conv_branch(u, dw_w, dw_b, g, bb, w_out):
    b, s, c = u.shape
    d = w_out.shape[1]
    ts = _tile(s, 256)
    hb = ts // CONV_HALO
    n_halo = s // CONV_HALO
    return pl.pallas_call(
        functools.partial(_conv_kernel, ts=ts),
        grid=(b, s // ts),
        in_specs=[pl.BlockSpec((1, ts, c), lambda i, j: (i, j, 0)),
                  pl.BlockSpec((1, CONV_HALO, c), lambda i, j: (i, jnp.maximum(j * hb - 1, 0), 0)),
                  pl.BlockSpec((1, CONV_HALO, c), lambda i, j: (i, jnp.minimum((j + 1) * hb, n_halo - 1), 0)),
                  _resident((CONV_WIDTH, c)), _resident((1, c)), _resident((1, c)), _resident((1, c)),
                  _resident(w_out.shape)],
        out_specs=pl.BlockSpec((1, ts, d), lambda i, j: (i, j, 0)),
        out_shape=jax.ShapeDtypeStruct((b, s, d), F32),
        scratch_shapes=[pltpu.VMEM((ts + 2 * CONV_HALO, c), F32), pltpu.VMEM((ts, c), F32)],
        compiler_params=_params(("arbitrary", "arbitrary"), 40),
        name="conv_branch",
    )(u, u, u, dw_w, dw_b.reshape(1, c), g.reshape(1, c), bb.reshape(1, c), w_out)


def _merge_kernel(attn_ref, conv_ref, gate_ref, x_ref, ga_ref, g_ref, b_ref, wmla_ref, wout_ref, o_ref):
    d = x_ref.shape[2]
    attn = _dot(attn_ref[0], wmla_ref[...])
    merged = gate_ref[0, :, :d].astype(F32) * conv_ref[0] + gate_ref[0, :, d:].astype(F32) * attn
    out = _dot(merged.astype(BF16), wout_ref[...])
    y = DEEPNORM_ALPHA * x_ref[0] + ga_ref[0] * out
    o_ref[0] = _ln(y) * g_ref[...] + b_ref[...]


def _merge(attn, conv, gates, x, gate_a, g, bb, w_mla, w_out):
    b, s, d = x.shape
    ts = _tile(s, 256)
    row = lambda n: pl.BlockSpec((1, ts, n), lambda i, j: (i, j, 0))
    return pl.pallas_call(
        _merge_kernel,
        grid=(b, s // ts),
        in_specs=[row(attn.shape[2]), row(d), row(2 * d), row(d),
                  pl.BlockSpec((1, 1, d), lambda i, j: (i, 0, 0)),
                  _resident((1, d)), _resident((1, d)), _resident(w_mla.shape), _resident(w_out.shape)],
        out_specs=row(d),
        out_shape=jax.ShapeDtypeStruct((b, s, d), F32),
        compiler_params=_params(("arbitrary", "arbitrary"), 48),
        name="merge",
    )(attn, conv, gates, x, gate_a, g.reshape(1, d), bb.reshape(1, d), w_mla, w_out)


def _mlp_kernel(x_ref, sh_ref, sc_ref, gm_ref, g_ref, b_ref, wu_ref, wd_ref, o_ref, h_scr, acc_scr):
    k = pl.program_id(2)

    @pl.when(k == 0)
    def _():
        h_scr[...] = (_ln(x_ref[0]) * (1.0 + sc_ref[0]) + sh_ref[0]).astype(BF16)
        acc_scr[...] = jnp.zeros_like(acc_scr)

    a = jnp.maximum(_dot(h_scr[...], wu_ref[...]), 0.0)
    acc_scr[...] += _dot((a * a).astype(BF16), wd_ref[...])

    @pl.when(k == pl.num_programs(2) - 1)
    def _():
        y = DEEPNORM_ALPHA * x_ref[0] + gm_ref[0] * acc_scr[...]
        o_ref[0] = _ln(y) * g_ref[...] + b_ref[...]


def _mlp(x, shift, scale, gate_m, g, bb, w_up, w_down):
    b, s, d = x.shape
    dff = w_up.shape[1]
    tm = _tile(s, 512)
    tk = _tile(dff, 1024)
    row = pl.BlockSpec((1, tm, d), lambda i, j, k: (i, j, 0))
    mod = pl.BlockSpec((1, 1, d), lambda i, j, k: (i, 0, 0))
    vec = pl.BlockSpec((1, d), lambda i, j, k: (0, 0))
    return pl.pallas_call(
        _mlp_kernel,
        grid=(b, s // tm, dff // tk),
        in_specs=[row, mod, mod, mod, vec, vec,
                  pl.BlockSpec((d, tk), lambda i, j, k: (0, k)),
                  pl.BlockSpec((tk, d), lambda i, j, k: (k, 0))],
        out_specs=row,
        out_shape=jax.ShapeDtypeStruct((b, s, d), F32),
        scratch_shapes=[pltpu.VMEM((tm, d), BF16), pltpu.VMEM((tm, d), F32)],
        compiler_params=_params(("arbitrary", "arbitrary", "arbitrary"), 48),
        name="mlp",
    )(x, shift, scale, gate_m, g.reshape(1, d), bb.reshape(1, d), w_up, w_down)


def _rope_tables(n_tok):
    rows = n_tok // GRID_W
    row = jnp.repeat(jnp.arange(rows, dtype=F32), GRID_W)
    col = jnp.tile(jnp.arange(GRID_W, dtype=F32), rows)
    inv_freq = ROPE_THETA ** (-jnp.arange(ROPE_FREQS, dtype=F32) / ROPE_FREQS)
    ang = jnp.stack([row[:, None] * inv_freq, col[:, None] * inv_freq], axis=1)
    cos, sin = jnp.cos(ang), jnp.sin(ang)
    c = jnp.stack([cos, cos], axis=2).reshape(n_tok, QK_ROPE_DIM)
    s = jnp.stack([-sin, sin], axis=2).reshape(n_tok, QK_ROPE_DIM)
    return jnp.concatenate([c, c], axis=1), jnp.concatenate([s, s], axis=1)


def _swap_halves(w):
    lead = w.shape[:-1]
    return w.reshape(lead + (2, 2, ROPE_FREQS))[..., ::-1, :].reshape(lead + (QK_ROPE_DIM,))


def kernel(x, c, ctx, c_ctx, w_ada, b_ada, w_in, dw_conv_w, dw_conv_b, conv_norm_g, conv_norm_b, w_conv_out, q_norm_g, kv_norm_g, w_uq, w_ukv, w_mla_out, w_out, ln_mix_g, ln_mix_b, w_mlp_up, w_mlp_down, ln_mlp_g, ln_mlp_b):
    b, s, d = x.shape
    lc = ctx.shape[1]
    kv_rank = kv_norm_g.shape[-1]
    q_rank = q_norm_g.shape[-1]
    conv_c = dw_conv_w.shape[-1]
    heads = w_uq.shape[-1] // (QK_NOPE_DIM + QK_ROPE_DIM)
    assert w_ada.shape[0] == DEPTH and heads % 2 == 0

    n_rows = -(-(b + 1) // 8) * 8
    c_rows = jnp.zeros((n_rows, d), F32).at[:b].set(c).at[b].set(c_ctx)
    mod = _ada(c_rows, w_ada[0], b_ada[0])
    mod_x = [mod[:b, i * d:(i + 1) * d].reshape(b, 1, d) for i in range(N_MOD)]
    mod_c = [mod[b:b + 1, i * d:(i + 1) * d].reshape(1, 1, d) for i in range(2)]

    wi = w_in[0]
    o_kr, o_q, o_glu, o_gate = kv_rank, kv_rank + QK_ROPE_DIM, kv_rank + QK_ROPE_DIM + q_rank, kv_rank + QK_ROPE_DIM + q_rank + 2 * conv_c
    w_kr = wi[:, o_kr:o_q]
    w_kr_sw = _swap_halves(w_kr)
    w_kv = jnp.concatenate([wi[:, :o_kr], w_kr, w_kr, w_kr_sw, w_kr_sw], axis=1).astype(BF16)
    w_q = wi[:, o_q:o_glu].astype(BF16)
    w_glu = wi[:, o_glu:o_gate].astype(BF16)
    w_gate = wi[:, o_gate:].astype(BF16)

    c2, s2 = _rope_tables(s)
    ones = jnp.ones((lc, LANES), F32)
    kvn_c, kpe_c = _proj_in(ctx, mod_c[0], mod_c[1], ones, jnp.zeros_like(ones), w_kv, kv_norm_g[0])
    kvn_x, kpe_x, qn, u, gates = _proj_in(x, mod_x[0], mod_x[1], c2, s2, w_kv, kv_norm_g[0],
                                          w_q, q_norm_g[0], w_glu, w_gate)

    wkv3 = w_ukv[0].reshape(kv_rank, heads, QK_NOPE_DIM + V_HEAD_DIM)
    w_k = wkv3[:, :, :QK_NOPE_DIM].reshape(kv_rank, heads * QK_NOPE_DIM).astype(BF16)
    w_v = wkv3[:, :, QK_NOPE_DIM:].reshape(kv_rank, heads * V_HEAD_DIM).astype(BF16)
    k_all, v_all = _kv_up(kvn_c, kpe_c, kvn_x, kpe_x, w_k, w_v, heads)

    wq3 = w_uq[0].reshape(q_rank, heads, QK_NOPE_DIM + QK_ROPE_DIM)
    w_qn = wq3[:, :, :QK_NOPE_DIM].reshape(q_rank, heads * QK_NOPE_DIM).astype(BF16)
    w_qr = wq3[:, :, QK_NOPE_DIM:]
    w_qs = _swap_halves(w_qr).reshape(q_rank, heads * QK_ROPE_DIM).astype(BF16)
    w_qr = w_qr.reshape(q_rank, heads * QK_ROPE_DIM).astype(BF16)
    q = _q_up(qn, c2, s2, w_qn, w_qr, w_qs, heads)

    attn = _attention(q, k_all, v_all)
    conv = _conv_branch(u, dw_conv_w[0], dw_conv_b[0], conv_norm_g[0], conv_norm_b[0], w_conv_out[0].astype(BF16))
    x1 = _merge(attn, conv, gates, x, mod_x[2], ln_mix_g[0], ln_mix_b[0],
                w_mla_out[0].astype(BF16), w_out[0].astype(BF16))
    return _mlp(x1, mod_x[3], mod_x[4], mod_x[5], ln_mlp_g[0], ln_mlp_b[0],
                w_mlp_up[0].astype(BF16), w_mlp_down[0].astype(BF16))
```

```python
import functools

import jax
import jax.numpy as jnp
from jax import lax
from jax.experimental import pallas as pl
from jax.experimental.pallas import tpu as pltpu

F32 = jnp.float32
BF16 = jnp.bfloat16

GRID_W = 64
QK_NOPE_DIM = 128
QK_ROPE_DIM = 64
V_HEAD_DIM = 128
QK_PAD_DIM = 256
ROPE_THETA = 10000.0
ROPE_FREQS = QK_ROPE_DIM // 4
SOFTMAX_SCALE = (QK_NOPE_DIM + QK_ROPE_DIM) ** -0.5
LOG2_E = 1.4426950408889634
Q_SCALE = SOFTMAX_SCALE * LOG2_E
CONV_WIDTH = 31
CONV_PAD = CONV_WIDTH // 2
CONV_HALO = 16
N_MOD = 6
LN_EPS = 1e-6
DEPTH = 1
DEEPNORM_ALPHA = (2 * DEPTH) ** 0.25
LANES = 128
SUBLANES = 8
V7X_VMEM_BYTES = 64 * 1024 * 1024


def _params(semantics, vmem_mb, flags=None):
    return pltpu.CompilerParams(dimension_semantics=semantics, flags=flags,
                                vmem_limit_bytes=min(vmem_mb * 1024 * 1024, V7X_VMEM_BYTES - (4 << 20)))


def _resident(shape):
    nd = len(shape)
    return pl.BlockSpec(shape, lambda *_: (0,) * nd, pipeline_mode=pl.Buffered(1))


def _tile(n, want):
    t = min(n, want)
    assert n % t == 0, (n, t)
    return t


def _ln(x):
    xc = x - jnp.mean(x, axis=-1, keepdims=True)
    return xc * lax.rsqrt(jnp.mean(xc * xc, axis=-1, keepdims=True) + LN_EPS)


def _rms(x, gain):
    return x * lax.rsqrt(jnp.mean(x * x, axis=-1, keepdims=True) + LN_EPS) * gain


def _sigmoid(x):
    return 1.0 / (1.0 + jnp.exp(-x))


def _dot(a, b):
    return jnp.dot(a, b, preferred_element_type=F32)


def _ada_kernel(c_ref, w_ref, b_ref, o_ref):
    c = c_ref[...]
    a = (c * _sigmoid(c)).astype(BF16)
    o_ref[...] = _dot(a, w_ref[...].astype(BF16)) + b_ref[...]


def _ada(c_rows, w_ada, b_ada):
    m, d = c_rows.shape
    n = w_ada.shape[1]
    tn = _tile(n, 1536)
    return pl.pallas_call(
        _ada_kernel,
        grid=(n // tn,),
        in_specs=[pl.BlockSpec((m, d), lambda j: (0, 0)),
                  pl.BlockSpec((d, tn), lambda j: (0, j)),
                  pl.BlockSpec((1, tn), lambda j: (0, j))],
        out_specs=pl.BlockSpec((m, tn), lambda j: (0, j)),
        out_shape=jax.ShapeDtypeStruct((m, n), F32),
        compiler_params=_params(("arbitrary",), 40),
        name="ada",
    )(c_rows, w_ada, b_ada.reshape(1, n))


def _proj_in_kernel(*refs, kv_rank, q_rank, conv_c, full):
    if full:
        x_ref, sh_ref, sc_ref, c2_ref, s2_ref, w_ref, kvg_ref, qg_ref, kvn_ref, kpe_ref, qn_ref, u_ref, gate_ref = refs
    else:
        x_ref, sh_ref, sc_ref, c2_ref, s2_ref, w_ref, kvg_ref, kvn_ref, kpe_ref = refs
    h = (_ln(x_ref[0]) * (1.0 + sc_ref[0]) + sh_ref[0]).astype(BF16)
    o_q = kv_rank + 2 * LANES
    zkv = _dot(h, w_ref[:, :o_q])
    kvn_ref[0] = _rms(zkv[:, :kv_rank], kvg_ref[...]).astype(BF16)
    kpe = zkv[:, kv_rank:kv_rank + LANES] * c2_ref[...] + zkv[:, kv_rank + LANES:o_q] * s2_ref[...]
    kpe_ref[0] = kpe.astype(BF16)
    if full:
        o_a, o_g, o_gate = o_q + q_rank, o_q + q_rank + conv_c, o_q + q_rank + 2 * conv_c
        qn_ref[0] = _rms(_dot(h, w_ref[:, o_q:o_a]), qg_ref[...]).astype(BF16)
        u_ref[0] = _dot(h, w_ref[:, o_a:o_g]) * _sigmoid(_dot(h, w_ref[:, o_g:o_gate]))
        ng = gate_ref.shape[2]
        step = min(ng, 1024)
        for c0 in range(0, ng, step):
            gate_ref[0, :, c0:c0 + step] = _sigmoid(_dot(h, w_ref[:, o_gate + c0:o_gate + c0 + step])).astype(BF16)


def _proj_in(x, shift, scale, c2, s2, w_all, kv_gain, q_gain, conv_c, full):
    b, l, d = x.shape
    ts = _tile(l, 256)
    kv_rank, q_rank = kv_gain.shape[-1], q_gain.shape[-1]
    o_q = kv_rank + 2 * LANES
    per_batch = shift.shape[0] == b and b > 1
    mod_spec = pl.BlockSpec((1, 1, d), (lambda i, j: (i, 0, 0)) if per_batch else (lambda i, j: (0, 0, 0)))
    row = lambda n: pl.BlockSpec((1, ts, n), lambda i, j: (i, j, 0))
    tab = pl.BlockSpec((ts, LANES), lambda i, j: (j, 0))
    w_spec = _resident(w_all.shape if full else (d, o_q))
    in_specs = [row(d), mod_spec, mod_spec, tab, tab, w_spec, _resident((1, kv_rank))]
    args = [x, shift, scale, c2, s2, w_all, kv_gain.reshape(1, kv_rank)]
    out_specs = [row(kv_rank), row(LANES)]
    out_shape = [jax.ShapeDtypeStruct((b, l, kv_rank), BF16), jax.ShapeDtypeStruct((b, l, LANES), BF16)]
    if full:
        ng = w_all.shape[1] - (o_q + q_rank + 2 * conv_c)
        in_specs += [_resident((1, q_rank))]
        args += [q_gain.reshape(1, q_rank)]
        out_specs += [row(q_rank), row(conv_c), row(ng)]
        out_shape += [jax.ShapeDtypeStruct((b, l, q_rank), BF16), jax.ShapeDtypeStruct((b, l, conv_c), F32),
                      jax.ShapeDtypeStruct((b, l, ng), BF16)]
    return pl.pallas_call(
        functools.partial(_proj_in_kernel, kv_rank=kv_rank, q_rank=q_rank, conv_c=conv_c, full=full),
        grid=(b, l // ts),
        in_specs=in_specs, out_specs=out_specs, out_shape=out_shape,
        compiler_params=_params(("arbitrary", "arbitrary"), 56),
        name="proj_in" if full else "proj_in_ctx",
    )(*args)


def _kv_up_kernel(kvn_c_ref, kpe_c_ref, kvn_x_ref, kpe_x_ref, wk_ref, wv_ref, k_ref, v_ref, *, n_ctx_tiles, heads):
    is_ctx = pl.program_id(1) < n_ctx_tiles

    def emit(kvn, kpe):
        lane = lax.broadcasted_iota(jnp.int32, kpe.shape, 1)
        zero = jnp.zeros_like(kpe)
        kpe_even = jnp.where(lane < QK_ROPE_DIM, kpe, zero)
        kpe_odd = jnp.where(lane >= QK_ROPE_DIM, kpe, zero)
        kn = _dot(kvn, wk_ref[...]).astype(BF16)
        vv = _dot(kvn, wv_ref[...]).astype(BF16)
        for h in range(heads):
            k_ref[0, h, :, :QK_NOPE_DIM] = kn[:, h * QK_NOPE_DIM:(h + 1) * QK_NOPE_DIM]
            k_ref[0, h, :, QK_NOPE_DIM:] = kpe_even if h % 2 == 0 else kpe_odd
            v_ref[0, h] = vv[:, h * V_HEAD_DIM:(h + 1) * V_HEAD_DIM]

    @pl.when(is_ctx)
    def _():
        emit(kvn_c_ref[0], kpe_c_ref[0])

    @pl.when(jnp.logical_not(is_ctx))
    def _():
        emit(kvn_x_ref[0], kpe_x_ref[0])


def _kv_up(kvn_c, kpe_c, kvn_x, kpe_x, w_k, w_v, heads):
    b, lc, r = kvn_c.shape
    lx = kvn_x.shape[1]
    tl = _tile(lc, 256)
    assert lx % tl == 0
    nct = lc // tl
    lk = lc + lx
    ctx_map = lambda i, j: (i, jnp.minimum(j, nct - 1), 0)
    x_map = lambda i, j: (i, jnp.maximum(j - nct, 0), 0)
    return pl.pallas_call(
        functools.partial(_kv_up_kernel, n_ctx_tiles=nct, heads=heads),
        grid=(b, lk // tl),
        in_specs=[pl.BlockSpec((1, tl, r), ctx_map), pl.BlockSpec((1, tl, LANES), ctx_map),
                  pl.BlockSpec((1, tl, r), x_map), pl.BlockSpec((1, tl, LANES), x_map),
                  _resident(w_k.shape), _resident(w_v.shape)],
        out_specs=[pl.BlockSpec((1, heads, tl, QK_PAD_DIM), lambda i, j: (i, 0, j, 0)),
                   pl.BlockSpec((1, heads, tl, V_HEAD_DIM), lambda i, j: (i, 0, j, 0))],
        out_shape=[jax.ShapeDtypeStruct((b, heads, lk, QK_PAD_DIM), BF16),
                   jax.ShapeDtypeStruct((b, heads, lk, V_HEAD_DIM), BF16)],
        compiler_params=_params(("arbitrary", "arbitrary"), 40),
        name="kv_up",
    )(kvn_c, kpe_c, kvn_x, kpe_x, w_k, w_v)


def _q_up_kernel(qn_ref, c2_ref, s2_ref, wn_ref, wr_ref, ws_ref, q_ref, *, heads):
    qn = qn_ref[0]
    nope = _dot(qn, wn_ref[...])
    rope = _dot(qn, wr_ref[...])
    swap = _dot(qn, ws_ref[...])
    c2 = c2_ref[...]
    s2 = s2_ref[...]
    lane = lax.broadcasted_iota(jnp.int32, c2.shape, 1)
    for p in range(heads // 2):
        rot = (rope[:, p * LANES:(p + 1) * LANES] * c2 + swap[:, p * LANES:(p + 1) * LANES] * s2) * Q_SCALE
        rot_even = jnp.where(lane < QK_ROPE_DIM, rot, 0.0).astype(BF16)
        rot_odd = jnp.where(lane >= QK_ROPE_DIM, rot, 0.0).astype(BF16)
        for h, rot_h in ((2 * p, rot_even), (2 * p + 1, rot_odd)):
            q_ref[0, h, :, :QK_NOPE_DIM] = (nope[:, h * QK_NOPE_DIM:(h + 1) * QK_NOPE_DIM] * Q_SCALE).astype(BF16)
            q_ref[0, h, :, QK_NOPE_DIM:] = rot_h


def _q_up(qn, c2, s2, w_nope, w_rope, w_swap, heads):
    b, l, r = qn.shape
    tl = _tile(l, 256)
    tab = pl.BlockSpec((tl, LANES), lambda i, j: (j, 0))
    return pl.pallas_call(
        functools.partial(_q_up_kernel, heads=heads),
        grid=(b, l // tl),
        in_specs=[pl.BlockSpec((1, tl, r), lambda i, j: (i, j, 0)), tab, tab,
                  _resident(w_nope.shape), _resident(w_rope.shape), _resident(w_swap.shape)],
        out_specs=pl.BlockSpec((1, heads, tl, QK_PAD_DIM), lambda i, j: (i, 0, j, 0)),
        out_shape=jax.ShapeDtypeStruct((b, heads, l, QK_PAD_DIM), BF16),
        compiler_params=_params(("arbitrary", "arbitrary"), 40),
        name="q_up",
    )(qn, c2, s2, w_nope, w_rope, w_swap)


def _attention_kernel(q_ref, k_ref, v_ref, o_ref, *, tq):
    n_tiles = q_ref.shape[2] // tq

    def body(i, carry):
        r0 = pl.multiple_of(i * tq, tq)
        q = q_ref[0, 0, pl.ds(r0, tq), :]
        s = lax.dot_general(q, k_ref[0, 0], (((1,), (1,)), ((), ())), preferred_element_type=F32)
        p = jnp.exp2(s - jnp.max(s, axis=-1, keepdims=True))
        denom = jnp.sum(p, axis=-1, keepdims=True)
        o = _dot(p.astype(BF16), v_ref[0, 0]) / denom
        o_ref[0, pl.ds(r0, tq), :] = o.astype(BF16)
        return carry

    lax.fori_loop(0, n_tiles, body, 0, unroll=4)


def _attention(q, k, v):
    b, heads, s, _ = q.shape
    lk = k.shape[2]
    tq = _tile(s, 256)
    return pl.pallas_call(
        functools.partial(_attention_kernel, tq=tq),
        grid=(b, heads),
        in_specs=[pl.BlockSpec((1, 1, s, QK_PAD_DIM), lambda i, h: (i, h, 0, 0)),
                  pl.BlockSpec((1, 1, lk, QK_PAD_DIM), lambda i, h: (i, h, 0, 0)),
                  pl.BlockSpec((1, 1, lk, V_HEAD_DIM), lambda i, h: (i, h, 0, 0))],
        out_specs=pl.BlockSpec((1, s, V_HEAD_DIM), lambda i, h: (i, 0, h)),
        out_shape=jax.ShapeDtypeStruct((b, s, heads * V_HEAD_DIM), BF16),
        compiler_params=_params(("arbitrary", "arbitrary"), 48),
        name="attention",
    )(q, k, v)


def _mixer_out_kernel(u_ref, prev_ref, next_ref, attn_ref, gate_ref, x_ref, ga_ref, shm_ref, scm_ref,
                      dw_ref, db_ref, cg_ref, cb_ref, g_ref, b_ref, wco_ref, wmla_ref, wout_ref,
                      x1_ref, h_ref, ubuf, cbuf, *, ts):
    j = pl.program_id(1)
    last = pl.num_programs(1) - 1
    c = ubuf.shape[1]
    d = x_ref.shape[2]
    halo_zeros = jnp.zeros((CONV_HALO, c), F32)

    @pl.when(j > 0)
    def _():
        ubuf[0:CONV_HALO, :] = prev_ref[0]

    @pl.when(j == 0)
    def _():
        ubuf[0:CONV_HALO, :] = halo_zeros

    @pl.when(j < last)
    def _():
        ubuf[CONV_HALO + ts:, :] = next_ref[0]

    @pl.when(j == last)
    def _():
        ubuf[CONV_HALO + ts:, :] = halo_zeros

    ubuf[CONV_HALO:CONV_HALO + ts, :] = u_ref[0]
    rows = min(ts, 64)
    cols = min(c, 128)
    base = CONV_HALO - CONV_PAD
    for r0 in range(0, ts, rows):
        for c0 in range(0, c, cols):
            acc = jnp.broadcast_to(db_ref[:, c0:c0 + cols], (rows, cols))
            for r in range(SUBLANES):
                n = rows if r == 0 else rows + SUBLANES
                part = None
                for t in range(CONV_WIDTH):
                    if (base + t) % SUBLANES != r:
                        continue
                    a = r0 + base + t - r
                    term = ubuf[a:a + n, c0:c0 + cols] * dw_ref[t:t + 1, c0:c0 + cols]
                    part = term if part is None else part + term
                if r:
                    part = pltpu.roll(part, n - r, axis=0)[:rows]
                acc = acc + part
            cbuf[r0:r0 + rows, c0:c0 + cols] = acc
    y = _ln(cbuf[...]) * cg_ref[...] + cb_ref[...]
    conv = _dot((y * _sigmoid(y)).astype(BF16), wco_ref[...])
    attn = _dot(attn_ref[0], wmla_ref[...])
    merged = gate_ref[0, :, :d].astype(F32) * conv + gate_ref[0, :, d:].astype(F32) * attn
    out = _dot(merged.astype(BF16), wout_ref[...])
    x1 = _ln(DEEPNORM_ALPHA * x_ref[0] + ga_ref[0] * out) * g_ref[...] + b_ref[...]
    x1_ref[0] = x1
    h_ref[0] = (_ln(x1) * (1.0 + scm_ref[0]) + shm_ref[0]).astype(BF16)


def _mixer_out(u, attn, gates, x, gate_a, shift_m, scale_m, dw_w, dw_b, conv_g, conv_b, g, bb, w_conv_out, w_mla, w_out):
    b, s, d = x.shape
    c = u.shape[2]
    ts = _tile(s, 256)
    hb = ts // CONV_HALO
    n_halo = s // CONV_HALO
    row = lambda n: pl.BlockSpec((1, ts, n), lambda i, j: (i, j, 0))
    mod = pl.BlockSpec((1, 1, d), lambda i, j: (i, 0, 0))
    return pl.pallas_call(
        functools.partial(_mixer_out_kernel, ts=ts),
        grid=(b, s // ts),
        in_specs=[row(c),
                  pl.BlockSpec((1, CONV_HALO, c), lambda i, j: (i, jnp.maximum(j * hb - 1, 0), 0)),
                  pl.BlockSpec((1, CONV_HALO, c), lambda i, j: (i, jnp.minimum((j + 1) * hb, n_halo - 1), 0)),
                  row(attn.shape[2]), row(2 * d), row(d), mod, mod, mod,
                  _resident((CONV_WIDTH, c)), _resident((1, c)), _resident((1, c)), _resident((1, c)),
                  _resident((1, d)), _resident((1, d)),
                  _resident(w_conv_out.shape), _resident(w_mla.shape), _resident(w_out.shape)],
        out_specs=[row(d), row(d)],
        out_shape=[jax.ShapeDtypeStruct((b, s, d), F32), jax.ShapeDtypeStruct((b, s, d), BF16)],
        scratch_shapes=[pltpu.VMEM((ts + 2 * CONV_HALO, c), F32), pltpu.VMEM((ts, c), F32)],
        compiler_params=_params(("arbitrary", "arbitrary"), 58),
        name="mixer_out",
    )(u, u, u, attn, gates, x, gate_a, shift_m, scale_m, dw_w, dw_b.reshape(1, c), conv_g.reshape(1, c),
      conv_b.reshape(1, c), g.reshape(1, d), bb.reshape(1, d), w_conv_out, w_mla, w_out)


def _mlp_kernel(x_ref, h_ref, gm_ref, g_ref, b_ref, wu_ref, wd_ref, o_ref, acc_scr):
    k = pl.program_id(2)

    @pl.when(k == 0)
    def _():
        acc_scr[...] = jnp.zeros_like(acc_scr)

    a = jnp.maximum(_dot(h_ref[0], wu_ref[...]), 0.0)
    acc_scr[...] += _dot((a * a).astype(BF16), wd_ref[...])

    @pl.when(k == pl.num_programs(2) - 1)
    def _():
        y = DEEPNORM_ALPHA * x_ref[0] + gm_ref[0] * acc_scr[...]
        o_ref[0] = _ln(y) * g_ref[...] + b_ref[...]


def _mlp(x, h, gate_m, g, bb, w_up, w_down):
    b, s, d = x.shape
    dff = w_up.shape[1]
    tm = _tile(s, 512)
    tk = _tile(dff, 1024)
    row = pl.BlockSpec((1, tm, d), lambda i, j, k: (i, j, 0))
    mod = pl.BlockSpec((1, 1, d), lambda i, j, k: (i, 0, 0))
    vec = pl.BlockSpec((1, d), lambda i, j, k: (0, 0))
    return pl.pallas_call(
        _mlp_kernel,
        grid=(b, s // tm, dff // tk),
        in_specs=[row, row, mod, vec, vec,
                  pl.BlockSpec((d, tk), lambda i, j, k: (0, k)),
                  pl.BlockSpec((tk, d), lambda i, j, k: (k, 0))],
        out_specs=row,
        out_shape=jax.ShapeDtypeStruct((b, s, d), F32),
        scratch_shapes=[pltpu.VMEM((tm, d), F32)],
        compiler_params=_params(("arbitrary", "arbitrary", "arbitrary"), 48),
        name="mlp",
    )(x, h, gate_m, g.reshape(1, d), bb.reshape(1, d), w_up, w_down)


def _rope_tables(n_tok):
    rows = n_tok // GRID_W
    row = jnp.repeat(jnp.arange(rows, dtype=F32), GRID_W)
    col = jnp.tile(jnp.arange(GRID_W, dtype=F32), rows)
    inv_freq = ROPE_THETA ** (-jnp.arange(ROPE_FREQS, dtype=F32) / ROPE_FREQS)
    ang = jnp.stack([row[:, None] * inv_freq, col[:, None] * inv_freq], axis=1)
    cos, sin = jnp.cos(ang), jnp.sin(ang)
    c = jnp.stack([cos, cos], axis=2).reshape(n_tok, QK_ROPE_DIM)
    s = jnp.stack([-sin, sin], axis=2).reshape(n_tok, QK_ROPE_DIM)
    return jnp.concatenate([c, c], axis=1), jnp.concatenate([s, s], axis=1)


def _swap_halves(w):
    lead = w.shape[:-1]
    return w.reshape(lead + (2, 2, ROPE_FREQS))[..., ::-1, :].reshape(lead + (QK_ROPE_DIM,))


def kernel(x, c, ctx, c_ctx, w_ada, b_ada, w_in, dw_conv_w, dw_conv_b, conv_norm_g, conv_norm_b, w_conv_out, q_norm_g, kv_norm_g, w_uq, w_ukv, w_mla_out, w_out, ln_mix_g, ln_mix_b, w_mlp_up, w_mlp_down, ln_mlp_g, ln_mlp_b):
    b, s, d = x.shape
    lc = ctx.shape[1]
    kv_rank = kv_norm_g.shape[-1]
    q_rank = q_norm_g.shape[-1]
    conv_c = dw_conv_w.shape[-1]
    heads = w_uq.shape[-1] // (QK_NOPE_DIM + QK_ROPE_DIM)
    assert w_ada.shape[0] == DEPTH and heads % 2 == 0

    n_rows = -(-(b + 1) // 8) * 8
    c_rows = jnp.zeros((n_rows, d), F32).at[:b].set(c).at[b].set(c_ctx)
    mod = _ada(c_rows, w_ada[0], b_ada[0])
    mod_x = [mod[:b, i * d:(i + 1) * d].reshape(b, 1, d) for i in range(N_MOD)]
    mod_c = [mod[b:b + 1, i * d:(i + 1) * d].reshape(1, 1, d) for i in range(2)]

    wi = w_in[0]
    o_kr, o_q = kv_rank, kv_rank + QK_ROPE_DIM
    w_kr = wi[:, o_kr:o_q]
    w_kr_sw = _swap_halves(w_kr)
    w_all = jnp.concatenate([wi[:, :o_kr], w_kr, w_kr, w_kr_sw, w_kr_sw, wi[:, o_q:]], axis=1).astype(BF16)

    c2, s2 = _rope_tables(s)
    ones = jnp.ones((lc, LANES), F32)
    kvn_c, kpe_c = _proj_in(ctx, mod_c[0], mod_c[1], ones, jnp.zeros_like(ones), w_all, kv_norm_g[0], q_norm_g[0],
                            conv_c, full=False)
    kvn_x, kpe_x, qn, u, gates = _proj_in(x, mod_x[0], mod_x[1], c2, s2, w_all, kv_norm_g[0], q_norm_g[0],
                                          conv_c, full=True)

    wkv3 = w_ukv[0].reshape(kv_rank, heads, QK_NOPE_DIM + V_HEAD_DIM)
    w_k = wkv3[:, :, :QK_NOPE_DIM].reshape(kv_rank, heads * QK_NOPE_DIM).astype(BF16)
    w_v = wkv3[:, :, QK_NOPE_DIM:].reshape(kv_rank, heads * V_HEAD_DIM).astype(BF16)
    k_all, v_all = _kv_up(kvn_c, kpe_c, kvn_x, kpe_x, w_k, w_v, heads)

    wq3 = w_uq[0].reshape(q_rank, heads, QK_NOPE_DIM + QK_ROPE_DIM)
    w_qn = wq3[:, :, :QK_NOPE_DIM].reshape(q_rank, heads * QK_NOPE_DIM).astype(BF16)
    w_qr = wq3[:, :, QK_NOPE_DIM:]
    w_qs = _swap_halves(w_qr).reshape(q_rank, heads * QK_ROPE_DIM).astype(BF16)
    w_qr = w_qr.reshape(q_rank, heads * QK_ROPE_DIM).astype(BF16)
    q = _q_up(qn, c2, s2, w_qn, w_qr, w_qs, heads)

    attn = _attention(q, k_all, v_all)
    x1, h_mlp = _mixer_out(u, attn, gates, x, mod_x[2], mod_x[3], mod_x[4], dw_conv_w[0], dw_conv_b[0],
                           conv_norm_g[0], conv_norm_b[0], ln_mix_g[0], ln_mix_b[0],
                           w_conv_out[0].astype(BF16), w_mla_out[0].astype(BF16), w_out[0].astype(BF16))
    return _mlp(x1, h_mlp, mod_x[5], ln_mlp_g[0], ln_mlp_b[0], w_mlp_up[0].astype(BF16), w_mlp_down[0].astype(BF16))
```

```python
import functools

import jax
import jax.numpy as jnp
from jax import lax
from jax.experimental import pallas as pl
from jax.experimental.pallas import tpu as pltpu

F32 = jnp.float32
BF16 = jnp.bfloat16

GRID_W = 64
QK_NOPE_DIM = 128
QK_ROPE_DIM = 64
V_HEAD_DIM = 128
QK_PAD_DIM = 256
ROPE_THETA = 10000.0
ROPE_FREQS = QK_ROPE_DIM // 4
SOFTMAX_SCALE = (QK_NOPE_DIM + QK_ROPE_DIM) ** -0.5
LOG2_E = 1.4426950408889634
Q_SCALE = SOFTMAX_SCALE * LOG2_E
CONV_WIDTH = 31
CONV_PAD = CONV_WIDTH // 2
CONV_HALO = 16
N_MOD = 6
LN_EPS = 1e-6
DEPTH = 1
DEEPNORM_ALPHA = (2 * DEPTH) ** 0.25
LANES = 128
SUBLANES = 8
V7X_VMEM_BYTES = 64 * 1024 * 1024


def _params(semantics, vmem_mb, flags=None):
    return pltpu.CompilerParams(dimension_semantics=semantics, flags=flags,
                                vmem_limit_bytes=min(vmem_mb * 1024 * 1024, V7X_VMEM_BYTES - (4 << 20)))


def _resident(shape):
    nd = len(shape)
    return pl.BlockSpec(shape, lambda *_: (0,) * nd, pipeline_mode=pl.Buffered(1))


def _tile(n, want):
    t = min(n, want)
    assert n % t == 0, (n, t)
    return t


def _ln(x):
    xc = x - jnp.mean(x, axis=-1, keepdims=True)
    return xc * lax.rsqrt(jnp.mean(xc * xc, axis=-1, keepdims=True) + LN_EPS)


def _rms(x, gain):
    return x * lax.rsqrt(jnp.mean(x * x, axis=-1, keepdims=True) + LN_EPS) * gain


def _sigmoid(x):
    return 1.0 / (1.0 + jnp.exp(-x))


def _dot(a, b):
    return jnp.dot(a, b, preferred_element_type=F32)


def _ada_kernel(c_ref, w_ref, b_ref, o_ref):
    c = c_ref[...]
    a = (c * _sigmoid(c)).astype(BF16)
    o_ref[...] = _dot(a, w_ref[...].astype(BF16)) + b_ref[...]


def _ada(c_rows, w_ada, b_ada):
    m, d = c_rows.shape
    n = w_ada.shape[1]
    tn = _tile(n, 1536)
    return pl.pallas_call(
        _ada_kernel,
        grid=(n // tn,),
        in_specs=[pl.BlockSpec((m, d), lambda j: (0, 0)),
                  pl.BlockSpec((d, tn), lambda j: (0, j)),
                  pl.BlockSpec((1, tn), lambda j: (0, j))],
        out_specs=pl.BlockSpec((m, tn), lambda j: (0, j)),
        out_shape=jax.ShapeDtypeStruct((m, n), F32),
        compiler_params=_params(("arbitrary",), 40),
        name="ada",
    )(c_rows, w_ada, b_ada.reshape(1, n))


def _proj_in_kernel(*refs, kv_rank, q_rank, conv_c, full):
    if full:
        (x_ref, sh_ref, sc_ref, c2_ref, s2_ref, wkv_ref, kvg_ref, w_ref, qg_ref,
         kvn_ref, kpe_ref, qn_ref, u_ref, gate_ref) = refs
    else:
        x_ref, sh_ref, sc_ref, c2_ref, s2_ref, wkv_ref, kvg_ref, kvn_ref, kpe_ref = refs
    h = (_ln(x_ref[0]) * (1.0 + sc_ref[0]) + sh_ref[0]).astype(BF16)
    zkv = _dot(h, wkv_ref[...])
    kvn_ref[0] = _rms(zkv[:, :kv_rank], kvg_ref[...]).astype(BF16)
    kpe = zkv[:, kv_rank:kv_rank + LANES] * c2_ref[...] + zkv[:, kv_rank + LANES:] * s2_ref[...]
    kpe_ref[0] = kpe.astype(BF16)
    if full:
        o_g, o_gate = q_rank + conv_c, q_rank + 2 * conv_c
        qn_ref[0] = _rms(_dot(h, w_ref[:, :q_rank]), qg_ref[...]).astype(BF16)
        u_ref[0] = _dot(h, w_ref[:, q_rank:o_g]) * _sigmoid(_dot(h, w_ref[:, o_g:o_gate]))
        ng = gate_ref.shape[2]
        step = min(ng, 1024)
        for c0 in range(0, ng, step):
            gate_ref[0, :, c0:c0 + step] = _sigmoid(_dot(h, w_ref[:, o_gate + c0:o_gate + c0 + step])).astype(BF16)


def _proj_in(x, shift, scale, c2, s2, w_kv, kv_gain, w_rest=None, q_gain=None, conv_c=0):
    b, l, d = x.shape
    full = w_rest is not None
    ts = _tile(l, 256)
    kv_rank = kv_gain.shape[-1]
    q_rank = q_gain.shape[-1] if full else 0
    per_batch = shift.shape[0] == b and b > 1
    mod_spec = pl.BlockSpec((1, 1, d), (lambda i, j: (i, 0, 0)) if per_batch else (lambda i, j: (0, 0, 0)))
    row = lambda n: pl.BlockSpec((1, ts, n), lambda i, j: (i, j, 0))
    tab = pl.BlockSpec((ts, LANES), lambda i, j: (j, 0))
    in_specs = [row(d), mod_spec, mod_spec, tab, tab, _resident(w_kv.shape), _resident((1, kv_rank))]
    args = [x, shift, scale, c2, s2, w_kv, kv_gain.reshape(1, kv_rank)]
    out_specs = [row(kv_rank), row(LANES)]
    out_shape = [jax.ShapeDtypeStruct((b, l, kv_rank), BF16), jax.ShapeDtypeStruct((b, l, LANES), BF16)]
    if full:
        ng = w_rest.shape[1] - (q_rank + 2 * conv_c)
        in_specs += [_resident(w_rest.shape), _resident((1, q_rank))]
        args += [w_rest, q_gain.reshape(1, q_rank)]
        out_specs += [row(q_rank), row(conv_c), row(ng)]
        out_shape += [jax.ShapeDtypeStruct((b, l, q_rank), BF16), jax.ShapeDtypeStruct((b, l, conv_c), F32),
                      jax.ShapeDtypeStruct((b, l, ng), BF16)]
    return pl.pallas_call(
        functools.partial(_proj_in_kernel, kv_rank=kv_rank, q_rank=q_rank, conv_c=conv_c, full=full),
        grid=(b, l // ts),
        in_specs=in_specs, out_specs=out_specs, out_shape=out_shape,
        compiler_params=_params(("arbitrary", "arbitrary"), 56),
        name="proj_in" if full else "proj_in_ctx",
    )(*args)


def _kv_up_kernel(kvn_c_ref, kpe_c_ref, kvn_x_ref, kpe_x_ref, wk_ref, wv_ref, k_ref, v_ref, *, n_ctx_tiles, heads):
    is_ctx = pl.program_id(1) < n_ctx_tiles

    def emit(kvn, kpe):
        lane = lax.broadcasted_iota(jnp.int32, kpe.shape, 1)
        zero = jnp.zeros_like(kpe)
        kpe_even = jnp.where(lane < QK_ROPE_DIM, kpe, zero)
        kpe_odd = jnp.where(lane >= QK_ROPE_DIM, kpe, zero)
        kn = _dot(kvn, wk_ref[...]).astype(BF16)
        vv = _dot(kvn, wv_ref[...]).astype(BF16)
        for h in range(heads):
            k_ref[0, h, :, :QK_NOPE_DIM] = kn[:, h * QK_NOPE_DIM:(h + 1) * QK_NOPE_DIM]
            k_ref[0, h, :, QK_NOPE_DIM:] = kpe_even if h % 2 == 0 else kpe_odd
            v_ref[0, h] = vv[:, h * V_HEAD_DIM:(h + 1) * V_HEAD_DIM]

    @pl.when(is_ctx)
    def _():
        emit(kvn_c_ref[0], kpe_c_ref[0])

    @pl.when(jnp.logical_not(is_ctx))
    def _():
        emit(kvn_x_ref[0], kpe_x_ref[0])


def _kv_up(kvn_c, kpe_c, kvn_x, kpe_x, w_k, w_v, heads):
    b, lc, r = kvn_c.shape
    lx = kvn_x.shape[1]
    tl = _tile(lc, 256)
    assert lx % tl == 0
    nct = lc // tl
    lk = lc + lx
    ctx_map = lambda i, j: (i, jnp.minimum(j, nct - 1), 0)
    x_map = lambda i, j: (i, jnp.maximum(j - nct, 0), 0)
    return pl.pallas_call(
        functools.partial(_kv_up_kernel, n_ctx_tiles=nct, heads=heads),
        grid=(b, lk // tl),
        in_specs=[pl.BlockSpec((1, tl, r), ctx_map), pl.BlockSpec((1, tl, LANES), ctx_map),
                  pl.BlockSpec((1, tl, r), x_map), pl.BlockSpec((1, tl, LANES), x_map),
                  _resident(w_k.shape), _resident(w_v.shape)],
        out_specs=[pl.BlockSpec((1, heads, tl, QK_PAD_DIM), lambda i, j: (i, 0, j, 0)),
                   pl.BlockSpec((1, heads, tl, V_HEAD_DIM), lambda i, j: (i, 0, j, 0))],
        out_shape=[jax.ShapeDtypeStruct((b, heads, lk, QK_PAD_DIM), BF16),
                   jax.ShapeDtypeStruct((b, heads, lk, V_HEAD_DIM), BF16)],
        compiler_params=_params(("arbitrary", "arbitrary"), 40),
        name="kv_up",
    )(kvn_c, kpe_c, kvn_x, kpe_x, w_k, w_v)


def _q_up_kernel(qn_ref, c2_ref, s2_ref, wn_ref, wr_ref, ws_ref, q_ref, *, heads):
    qn = qn_ref[0]
    nope = _dot(qn, wn_ref[...])
    rope = _dot(qn, wr_ref[...])
    swap = _dot(qn, ws_ref[...])
    c2 = c2_ref[...]
    s2 = s2_ref[...]
    lane = lax.broadcasted_iota(jnp.int32, c2.shape, 1)
    for p in range(heads // 2):
        rot = (rope[:, p * LANES:(p + 1) * LANES] * c2 + swap[:, p * LANES:(p + 1) * LANES] * s2) * Q_SCALE
        rot_even = jnp.where(lane < QK_ROPE_DIM, rot, 0.0).astype(BF16)
        rot_odd = jnp.where(lane >= QK_ROPE_DIM, rot, 0.0).astype(BF16)
        for h, rot_h in ((2 * p, rot_even), (2 * p + 1, rot_odd)):
            q_ref[0, h, :, :QK_NOPE_DIM] = (nope[:, h * QK_NOPE_DIM:(h + 1) * QK_NOPE_DIM] * Q_SCALE).astype(BF16)
            q_ref[0, h, :, QK_NOPE_DIM:] = rot_h


def _q_up(qn, c2, s2, w_nope, w_rope, w_swap, heads):
    b, l, r = qn.shape
    tl = _tile(l, 256)
    tab = pl.BlockSpec((tl, LANES), lambda i, j: (j, 0))
    return pl.pallas_call(
        functools.partial(_q_up_kernel, heads=heads),
        grid=(b, l // tl),
        in_specs=[pl.BlockSpec((1, tl, r), lambda i, j: (i, j, 0)), tab, tab,
                  _resident(w_nope.shape), _resident(w_rope.shape), _resident(w_swap.shape)],
        out_specs=pl.BlockSpec((1, heads, tl, QK_PAD_DIM), lambda i, j: (i, 0, j, 0)),
        out_shape=jax.ShapeDtypeStruct((b, heads, l, QK_PAD_DIM), BF16),
        compiler_params=_params(("arbitrary", "arbitrary"), 40),
        name="q_up",
    )(qn, c2, s2, w_nope, w_rope, w_swap)


def _attention_kernel(q_ref, k_ref, v_ref, o_ref, *, tq):
    n_tiles = q_ref.shape[2] // tq

    def body(i, carry):
        r0 = pl.multiple_of(i * tq, tq)
        q = q_ref[0, 0, pl.ds(r0, tq), :]
        s = lax.dot_general(q, k_ref[0, 0], (((1,), (1,)), ((), ())), preferred_element_type=F32)
        p = jnp.exp2(s - jnp.max(s, axis=-1, keepdims=True))
        denom = jnp.sum(p, axis=-1, keepdims=True)
        o = _dot(p.astype(BF16), v_ref[0, 0]) / denom
        o_ref[0, pl.ds(r0, tq), :] = o.astype(BF16)
        return carry

    lax.fori_loop(0, n_tiles, body, 0, unroll=8)


def _attention(q, k, v):
    b, heads, s, _ = q.shape
    lk = k.shape[2]
    tq = _tile(s, 256)
    return pl.pallas_call(
        functools.partial(_attention_kernel, tq=tq),
        grid=(b, heads),
        in_specs=[pl.BlockSpec((1, 1, s, QK_PAD_DIM), lambda i, h: (i, h, 0, 0)),
                  pl.BlockSpec((1, 1, lk, QK_PAD_DIM), lambda i, h: (i, h, 0, 0)),
                  pl.BlockSpec((1, 1, lk, V_HEAD_DIM), lambda i, h: (i, h, 0, 0))],
        out_specs=pl.BlockSpec((1, s, V_HEAD_DIM), lambda i, h: (i, 0, h)),
        out_shape=jax.ShapeDtypeStruct((b, s, heads * V_HEAD_DIM), BF16),
        compiler_params=_params(("arbitrary", "arbitrary"), 48),
        name="attention",
    )(q, k, v)


def _mixer_out_kernel(u_ref, prev_ref, next_ref, attn_ref, gate_ref, x_ref, ga_ref, shm_ref, scm_ref,
                      dw_ref, db_ref, cg_ref, cb_ref, g_ref, b_ref, wco_ref, wmla_ref, wout_ref,
                      x1_ref, h_ref, ubuf, cbuf, mbuf, *, ts, tiles_per_batch):
    i = pl.program_id(0)
    n_tiles = pl.num_programs(0) - 1
    ja = jnp.minimum(i, n_tiles - 1) % tiles_per_batch
    c = ubuf.shape[1]
    d = x_ref.shape[2]
    halo_zeros = jnp.zeros((CONV_HALO, c), F32)

    @pl.when(i == 0)
    def _():
        mbuf[...] = jnp.zeros_like(mbuf)

    @pl.when(ja > 0)
    def _():
        ubuf[0:CONV_HALO, :] = prev_ref[0]

    @pl.when(ja == 0)
    def _():
        ubuf[0:CONV_HALO, :] = halo_zeros

    @pl.when(ja < tiles_per_batch - 1)
    def _():
        ubuf[CONV_HALO + ts:, :] = next_ref[0]

    @pl.when(ja == tiles_per_batch - 1)
    def _():
        ubuf[CONV_HALO + ts:, :] = halo_zeros

    out = _dot(mbuf[...], wout_ref[...])
    x1 = _ln(DEEPNORM_ALPHA * x_ref[0] + ga_ref[0] * out) * g_ref[...] + b_ref[...]
    x1_ref[0] = x1
    h_ref[0] = (_ln(x1) * (1.0 + scm_ref[0]) + shm_ref[0]).astype(BF16)

    attn = _dot(attn_ref[0], wmla_ref[...])
    ubuf[CONV_HALO:CONV_HALO + ts, :] = u_ref[0]
    rows = min(ts, 64)
    cols = min(c, 128)
    base = CONV_HALO - CONV_PAD
    for r0 in range(0, ts, rows):
        for c0 in range(0, c, cols):
            acc = jnp.broadcast_to(db_ref[:, c0:c0 + cols], (rows, cols))
            for r in range(SUBLANES):
                n = rows if r == 0 else rows + SUBLANES
                part = None
                for t in range(CONV_WIDTH):
                    if (base + t) % SUBLANES != r:
                        continue
                    a = r0 + base + t - r
                    term = ubuf[a:a + n, c0:c0 + cols] * dw_ref[t:t + 1, c0:c0 + cols]
                    part = term if part is None else part + term
                if r:
                    part = pltpu.roll(part, n - r, axis=0)[:rows]
                acc = acc + part
            cbuf[r0:r0 + rows, c0:c0 + cols] = acc
    y = _ln(cbuf[...]) * cg_ref[...] + cb_ref[...]
    conv = _dot((y * _sigmoid(y)).astype(BF16), wco_ref[...])
    merged = gate_ref[0, :, :d].astype(F32) * conv + gate_ref[0, :, d:].astype(F32) * attn
    mbuf[...] = merged.astype(BF16)


def _mixer_out(u, attn, gates, x, gate_a, shift_m, scale_m, dw_w, dw_b, conv_g, conv_b, g, bb, w_conv_out, w_mla, w_out):
    b, s, d = x.shape
    c = u.shape[2]
    ts = _tile(s, 256)
    tpb = s // ts
    n_tiles = b * tpb
    hb = ts // CONV_HALO
    n_halo = s // CONV_HALO

    def cur(i):
        t = jnp.minimum(i, n_tiles - 1)
        return t // tpb, t % tpb

    def prv(i):
        t = jnp.maximum(i - 1, 0)
        return t // tpb, t % tpb

    def cur_map(i):
        bi, j = cur(i)
        return bi, j, 0

    def halo_before(i):
        bi, j = cur(i)
        return bi, jnp.maximum(j * hb - 1, 0), 0

    def halo_after(i):
        bi, j = cur(i)
        return bi, jnp.minimum((j + 1) * hb, n_halo - 1), 0

    def prv_map(i):
        bi, j = prv(i)
        return bi, j, 0

    def prv_mod(i):
        return prv(i)[0], 0, 0

    cur_row = lambda n: pl.BlockSpec((1, ts, n), cur_map)
    prv_row = lambda n: pl.BlockSpec((1, ts, n), prv_map)
    mod = pl.BlockSpec((1, 1, d), prv_mod)
    return pl.pallas_call(
        functools.partial(_mixer_out_kernel, ts=ts, tiles_per_batch=tpb),
        grid=(n_tiles + 1,),
        in_specs=[cur_row(c), pl.BlockSpec((1, CONV_HALO, c), halo_before), pl.BlockSpec((1, CONV_HALO, c), halo_after),
                  cur_row(attn.shape[2]), cur_row(2 * d), prv_row(d), mod, mod, mod,
                  _resident((CONV_WIDTH, c)), _resident((1, c)), _resident((1, c)), _resident((1, c)),
                  _resident((1, d)), _resident((1, d)),
                  _resident(w_conv_out.shape), _resident(w_mla.shape), _resident(w_out.shape)],
        out_specs=[prv_row(d), prv_row(d)],
        out_shape=[jax.ShapeDtypeStruct((b, s, d), F32), jax.ShapeDtypeStruct((b, s, d), BF16)],
        scratch_shapes=[pltpu.VMEM((ts + 2 * CONV_HALO, c), F32), pltpu.VMEM((ts, c), F32), pltpu.VMEM((ts, d), BF16)],
        compiler_params=_params(("arbitrary",), 58),
        name="mixer_out",
    )(u, u, u, attn, gates, x, gate_a, shift_m, scale_m, dw_w, dw_b.reshape(1, c), conv_g.reshape(1, c),
      conv_b.reshape(1, c), g.reshape(1, d), bb.reshape(1, d), w_conv_out, w_mla, w_out)


def _mlp_kernel(x_ref, h_ref, gm_ref, g_ref, b_ref, wu_ref, wd_ref, o_ref, acc_scr):
    k = pl.program_id(2)

    @pl.when(k == 0)
    def _():
        acc_scr[...] = jnp.zeros_like(acc_scr)

    a = jnp.maximum(_dot(h_ref[0], wu_ref[...]), 0.0)
    acc_scr[...] += _dot((a * a).astype(BF16), wd_ref[...])

    @pl.when(k == pl.num_programs(2) - 1)
    def _():
        y = DEEPNORM_ALPHA * x_ref[0] + gm_ref[0] * acc_scr[...]
        o_ref[0] = _ln(y) * g_ref[...] + b_ref[...]


def _mlp(x, h, gate_m, g, bb, w_up, w_down):
    b, s, d = x.shape
    dff = w_up.shape[1]
    tm = _tile(s, 512)
    tk = _tile(dff, 1024)
    row = pl.BlockSpec((1, tm, d), lambda i, j, k: (i, j, 0))
    mod = pl.BlockSpec((1, 1, d), lambda i, j, k: (i, 0, 0))
    vec = pl.BlockSpec((1, d), lambda i, j, k: (0, 0))
    return pl.pallas_call(
        _mlp_kernel,
        grid=(b, s // tm, dff // tk),
        in_specs=[row, row, mod, vec, vec,
                  pl.BlockSpec((d, tk), lambda i, j, k: (0, k)),
                  pl.BlockSpec((tk, d), lambda i, j, k: (k, 0))],
        out_specs=row,
        out_shape=jax.ShapeDtypeStruct((b, s, d), F32),
        scratch_shapes=[pltpu.VMEM((tm, d), F32)],
        compiler_params=_params(("arbitrary", "arbitrary", "arbitrary"), 48),
        name="mlp",
    )(x, h, gate_m, g.reshape(1, d), bb.reshape(1, d), w_up, w_down)


def _rope_tables(n_tok):
    rows = n_tok // GRID_W
    row = jnp.repeat(jnp.arange(rows, dtype=F32), GRID_W)
    col = jnp.tile(jnp.arange(GRID_W, dtype=F32), rows)
    inv_freq = ROPE_THETA ** (-jnp.arange(ROPE_FREQS, dtype=F32) / ROPE_FREQS)
    ang = jnp.stack([row[:, None] * inv_freq, col[:, None] * inv_freq], axis=1)
    cos, sin = jnp.cos(ang), jnp.sin(ang)
    c = jnp.stack([cos, cos], axis=2).reshape(n_tok, QK_ROPE_DIM)
    s = jnp.stack([-sin, sin], axis=2).reshape(n_tok, QK_ROPE_DIM)
    return jnp.concatenate([c, c], axis=1), jnp.concatenate([s, s], axis=1)


def _swap_halves(w):
    lead = w.shape[:-1]
    return w.reshape(lead + (2, 2, ROPE_FREQS))[..., ::-1, :].reshape(lead + (QK_ROPE_DIM,))


def kernel(x, c, ctx, c_ctx, w_ada, b_ada, w_in, dw_conv_w, dw_conv_b, conv_norm_g, conv_norm_b, w_conv_out, q_norm_g, kv_norm_g, w_uq, w_ukv, w_mla_out, w_out, ln_mix_g, ln_mix_b, w_mlp_up, w_mlp_down, ln_mlp_g, ln_mlp_b):
    b, s, d = x.shape
    lc = ctx.shape[1]
    kv_rank = kv_norm_g.shape[-1]
    q_rank = q_norm_g.shape[-1]
    conv_c = dw_conv_w.shape[-1]
    heads = w_uq.shape[-1] // (QK_NOPE_DIM + QK_ROPE_DIM)
    assert w_ada.shape[0] == DEPTH and heads % 2 == 0

    n_rows = -(-(b + 1) // 8) * 8
    c_rows = jnp.zeros((n_rows, d), F32).at[:b].set(c).at[b].set(c_ctx)
    mod = _ada(c_rows, w_ada[0], b_ada[0])
    mod_x = [mod[:b, i * d:(i + 1) * d].reshape(b, 1, d) for i in range(N_MOD)]
    mod_c = [mod[b:b + 1, i * d:(i + 1) * d].reshape(1, 1, d) for i in range(2)]

    wi = w_in[0]
    o_kr, o_q = kv_rank, kv_rank + QK_ROPE_DIM
    w_kr = wi[:, o_kr:o_q]
    w_kr_sw = _swap_halves(w_kr)
    w_kv = jnp.concatenate([wi[:, :o_kr], w_kr, w_kr, w_kr_sw, w_kr_sw], axis=1).astype(BF16)
    w_rest = wi[:, o_q:].astype(BF16)

    c2, s2 = _rope_tables(s)
    ones = jnp.ones((lc, LANES), F32)
    kvn_c, kpe_c = _proj_in(ctx, mod_c[0], mod_c[1], ones, jnp.zeros_like(ones), w_kv, kv_norm_g[0])
    kvn_x, kpe_x, qn, u, gates = _proj_in(x, mod_x[0], mod_x[1], c2, s2, w_kv, kv_norm_g[0],
                                          w_rest, q_norm_g[0], conv_c)

    wkv3 = w_ukv[0].reshape(kv_rank, heads, QK_NOPE_DIM + V_HEAD_DIM)
    w_k = wkv3[:, :, :QK_NOPE_DIM].reshape(kv_rank, heads * QK_NOPE_DIM).astype(BF16)
    w_v = wkv3[:, :, QK_NOPE_DIM:].reshape(kv_rank, heads * V_HEAD_DIM).astype(BF16)
    k_all, v_all = _kv_up(kvn_c, kpe_c, kvn_x, kpe_x, w_k, w_v, heads)

    wq3 = w_uq[0].reshape(q_rank, heads, QK_NOPE_DIM + QK_ROPE_DIM)
    w_qn = wq3[:, :, :QK_NOPE_DIM].reshape(q_rank, heads * QK_NOPE_DIM).astype(BF16)
    w_qr = wq3[:, :, QK_NOPE_DIM:]
    w_qs = _swap_halves(w_qr).reshape(q_rank, heads * QK_ROPE_DIM).astype(BF16)
    w_qr = w_qr.reshape(q_rank, heads * QK_ROPE_DIM).astype(BF16)
    q = _q_up(qn, c2, s2, w_qn, w_qr, w_qs, heads)

    attn = _attention(q, k_all, v_all)
    x1, h_mlp = _mixer_out(u, attn, gates, x, mod_x[2], mod_x[3], mod_x[4], dw_conv_w[0], dw_conv_b[0],
                           conv_norm_g[0], conv_norm_b[0], ln_mix_g[0], ln_mix_b[0],
                           w_conv_out[0].astype(BF16), w_mla_out[0].astype(BF16), w_out[0].astype(BF16))
    return _mlp(x1, h_mlp, mod_x[5], ln_mlp_g[0], ln_mlp_b[0], w_mlp_up[0].astype(BF16), w_mlp_down[0].astype(BF16))
```
